```python
import math
import jax, jax.numpy as jnp
from jax import lax
import numpy as np

D_MODEL = 1024
BATCH = 4
SEQ = 8192
DEPTH = 2
DEC_BATCH = 32
DEC_SEQ = 2048
PAST_LEN = 128

D_ATTN = D_MODEL // 2
D_POOL = D_MODEL // 4
D_GATE = D_MODEL // 4
D_MIX = D_ATTN + D_POOL + D_GATE
N_HEADS = 8
HEAD_DIM = D_ATTN // N_HEADS
DILATED_PATTERNS = ((128, 1), (512, 4), (2048, 16))
POOL_WINDOWS = (2, 4, 8, 16)
POOL_GROUP = D_POOL // len(POOL_WINDOWS)
CHUNK = 128
N_GATE_GROUPS = 4
GATE_GROUP = D_GATE // N_GATE_GROUPS
D_IN = 3 * D_ATTN + D_POOL + 2 * D_GATE
N_MEM = 256
N_CROSS_HEADS = 4
CROSS_HEAD_DIM = D_MODEL // N_CROSS_HEADS
N_GROUPS = 4
EXPERTS_PER_GROUP = 4
N_EXPERTS = N_GROUPS * EXPERTS_PER_GROUP
TOP_K_INNER = 2
D_EXPERT = 512
EPS = 1e-6
NEG_INF = -1e30

kernel_name = 'hybrid_dilated_pool_gmlp_hmoe_encoder'


def _rmsnorm(x, g):
    xf = x.astype(jnp.float32)
    y = xf * lax.rsqrt(jnp.mean(xf * xf, axis=-1, keepdims=True) + EPS)
    return (y * g.astype(jnp.float32)).astype(x.dtype)


def _alibi_slopes(n):
    return jnp.exp2(-8.0 * jnp.arange(1, n + 1, dtype=jnp.float32) / n)


def _banded_attention(q, k, v, radius, dist_scale, slopes):
    n, L, h, e = q.shape
    r = radius
    nb = -(-L // r)
    Lp = nb * r
    qb = jnp.pad(q, ((0, 0), (0, Lp - L), (0, 0), (0, 0))).reshape(n, nb, r, h, e)
    pad_k = ((0, 0), (r, Lp - L + r), (0, 0), (0, 0))

    def blocks(t):
        tb = jnp.pad(t, pad_k).reshape(n, nb + 2, r, h, e)
        return jnp.concatenate([tb[:, :-2], tb[:, 1:-1], tb[:, 2:]], axis=2)

    kb, vb = blocks(k), blocks(v)
    s = jnp.einsum('nbqhe,nbkhe->nbhqk', qb, kb, preferred_element_type=jnp.float32) * (e ** -0.5)
    rel = jnp.arange(3 * r)[None, :] - r - jnp.arange(r)[:, None]
    kpos = jnp.arange(nb)[:, None] * r - r + jnp.arange(3 * r)[None, :]
    valid = (jnp.abs(rel)[None] <= r) & ((kpos >= 0) & (kpos < L))[:, None, :]
    dist = (jnp.abs(rel) * dist_scale).astype(jnp.float32)
    s = s - slopes[None, None, :, None, None] * dist[None, None, None]
    s = jnp.where(valid[None, :, None], s, NEG_INF)
    m = jnp.max(s, axis=-1, keepdims=True)
    p = jnp.exp(s - m)
    den = jnp.sum(p, axis=-1)
    o = jnp.einsum('nbhqk,nbkhe->nbqhe', p, vb.astype(jnp.float32))
    o = o / jnp.transpose(den, (0, 1, 3, 2))[..., None]
    lse = jnp.transpose(m[..., 0] + jnp.log(den), (0, 1, 3, 2))
    return o.reshape(n, Lp, h, e)[:, :L], lse.reshape(n, Lp, h)[:, :L]


def _dilated_attention(q, k, v):
    b, S, h, e = q.shape
    slopes = _alibi_slopes(h)
    outs, lses = [], []
    for window, dil in DILATED_PATTERNS:
        L = S // dil

        def to_res(t):
            return jnp.transpose(t.reshape(b, L, dil, h, e), (0, 2, 1, 3, 4)).reshape(b * dil, L, h, e)

        o, lse = _banded_attention(to_res(q), to_res(k), to_res(v), window // (2 * dil), dil, slopes)
        outs.append(jnp.transpose(o.reshape(b, dil, L, h, e), (0, 2, 1, 3, 4)).reshape(b, S, h, e))
        lses.append(jnp.transpose(lse.reshape(b, dil, L, h), (0, 2, 1, 3)).reshape(b, S, h))
    alpha = jax.nn.softmax(jnp.stack(lses), axis=0)
    o = jnp.einsum('pbsh,pbshe->bshe', alpha, jnp.stack(outs))
    return o.astype(q.dtype).reshape(b, S, h * e)


def _multiscale_pool(x, w_pool, scale):
    b, S, c = x.shape
    ng = len(POOL_WINDOWS)
    xg = x.astype(jnp.float32).reshape(b, S, ng, POOL_GROUP)
    cs = jnp.concatenate([jnp.zeros((b, 1, ng, POOL_GROUP), jnp.float32), jnp.cumsum(xg, axis=1)], axis=1)
    t = jnp.arange(S)
    pooled = []
    for gi, w in enumerate(POOL_WINDOWS):
        lo = jnp.clip(t - w // 2, 0, S)
        hi = jnp.clip(t + w // 2, 0, S)
        csg = cs[:, :, gi]
        mean = (csg[:, hi] - csg[:, lo]) / (hi - lo).astype(jnp.float32)[None, :, None]
        pooled.append(mean - xg[:, :, gi])
    pm = jnp.stack(pooled, axis=2)
    y = jnp.einsum('bsgc,gcd->bsgd', pm, w_pool.astype(jnp.float32)).reshape(b, S, c)
    return (y * scale.astype(jnp.float32)).astype(x.dtype)


def _spatial_gate(u, v, gate_norm, w_s, b_s):
    b, S, c = u.shape
    u = jax.nn.gelu(u)
    v = _rmsnorm(jax.nn.gelu(v), gate_norm)
    vc = v.reshape(b, S // CHUNK, CHUNK, N_GATE_GROUPS, GATE_GROUP)
    f = jnp.einsum('gts,bnsgc->bntgc', w_s, vc) + jnp.transpose(b_s)[None, None, :, :, None]
    return u * f.reshape(b, S, c)


def _memory_cross_attention(h, mem_n, w_q, w_kv, w_o):
    b, S, d = h.shape
    m = mem_n.shape[1]
    q = (h @ w_q).reshape(b, S, N_CROSS_HEADS, CROSS_HEAD_DIM)
    kv = (mem_n @ w_kv).reshape(b, m, 2, N_CROSS_HEADS, CROSS_HEAD_DIM)
    s = jnp.einsum('bshe,bmhe->bhsm', q, kv[:, :, 0], preferred_element_type=jnp.float32) * (CROSS_HEAD_DIM ** -0.5)
    p = jax.nn.softmax(s, axis=-1)
    o = jnp.einsum('bhsm,bmhe->bshe', p, kv[:, :, 1].astype(jnp.float32)).astype(h.dtype)
    return o.reshape(b, S, d) @ w_o


def _hier_moe(h, w_rg, b_rg, w_re, b_re, w_up, w_down):
    b, S, d = h.shape
    xf = h.reshape(-1, d)
    lg = jnp.dot(xf, w_rg, preferred_element_type=jnp.float32) + b_rg.astype(jnp.float32)
    pg = jax.nn.softmax(lg, axis=-1)
    g_idx = jnp.argmax(pg, axis=-1)
    g_w = jnp.max(pg, axis=-1)
    le = (jnp.dot(xf, w_re, preferred_element_type=jnp.float32) + b_re.astype(jnp.float32))
    le = le.reshape(-1, N_GROUPS, EXPERTS_PER_GROUP)
    le_sel = jnp.take_along_axis(le, g_idx[:, None, None], axis=1)[:, 0]
    top_v, top_i = lax.top_k(le_sel, TOP_K_INNER)
    g_inner = jax.nn.softmax(top_v, axis=-1) * g_w[:, None]
    eid = g_idx[:, None] * EXPERTS_PER_GROUP + top_i
    gates = jnp.sum(jax.nn.one_hot(eid, N_EXPERTS, dtype=jnp.float32) * g_inner[..., None], axis=1)
    y = jnp.zeros(xf.shape, jnp.float32)
    for e in range(N_EXPERTS):
        gu = xf @ w_up[e]
        a = jax.nn.silu(gu[:, :D_EXPERT]) * gu[:, D_EXPERT:]
        y = y + gates[:, e:e + 1] * jnp.dot(a, w_down[e], preferred_element_type=jnp.float32)
    return y.astype(h.dtype).reshape(b, S, d)


def _trunk(x, mem, p):
    cuts = [D_ATTN, 2 * D_ATTN, 3 * D_ATTN, 3 * D_ATTN + D_POOL, 3 * D_ATTN + D_POOL + D_GATE]
    for l in range(DEPTH):
        b, S, _ = x.shape
        h = _rmsnorm(x, p['norm_mix'][l])
        z = h @ p['w_in'][l]
        qa, ka, va, zb, uc, vc = jnp.split(z, cuts, axis=-1)
        hd = (b, S, N_HEADS, HEAD_DIM)
        ya = _dilated_attention(qa.reshape(hd), ka.reshape(hd), va.reshape(hd))
        yb = _multiscale_pool(zb, p['w_pool'][l], p['pool_scale'][l])
        yc = _spatial_gate(uc, vc, p['gate_norm'][l], p['w_spatial'][l], p['b_spatial'][l])
        x = x + jnp.concatenate([ya, yb, yc], axis=-1) @ p['w_out'][l]
        hc = _rmsnorm(x, p['norm_cross'][l])
        mem_n = _rmsnorm(mem, p['norm_mem'][l])
        x = x + _memory_cross_attention(hc, mem_n, p['w_cq'][l], p['w_ckv'][l], p['w_co'][l])
        hf = _rmsnorm(x, p['norm_ffn'][l])
        x = x + _hier_moe(hf, p['w_router_group'][l], p['b_router_group'][l], p['w_router_expert'][l],
                          p['b_router_expert'][l], p['w_up'][l], p['w_down'][l])
    return _rmsnorm(x, p['norm_final'])


def setup_inputs(seed: int = 0) -> dict:
    key = jax.random.key(seed)
    ks = jax.random.split(key, 28)
    f32 = jnp.float32

    def nrm(k, shape, scale):
        return jax.random.normal(k, shape, f32) * scale

    def gain(k, shape, base=1.0, noise=0.05):
        return base + noise * jax.random.normal(k, shape, f32)

    L, D = DEPTH, D_MODEL
    return {
        'x_prompt': nrm(ks[0], (BATCH, SEQ, D), 1.0),
        'x_sample': nrm(ks[1], (DEC_BATCH, DEC_SEQ, D), 1.0),
        'mem_prompt': nrm(ks[2], (BATCH, N_MEM, D), 1.0),
        'mem_sample': nrm(ks[3], (DEC_BATCH, N_MEM, D), 1.0),
        'norm_mix': gain(ks[4], (L, D)),
        'w_in': nrm(ks[5], (L, D, D_IN), D ** -0.5),
        'w_pool': nrm(ks[6], (L, len(POOL_WINDOWS), POOL_GROUP, POOL_GROUP), POOL_GROUP ** -0.5),
        'pool_scale': gain(ks[7], (L, D_POOL), 1.0, 0.1),
        'gate_norm': gain(ks[8], (L, D_GATE)),
        'w_spatial': nrm(ks[9], (L, N_GATE_GROUPS, CHUNK, CHUNK), CHUNK ** -0.5),
        'b_spatial': gain(ks[10], (L, N_GATE_GROUPS, CHUNK), 1.0, 0.1),
        'w_out': nrm(ks[11], (L, D_MIX, D), D_MIX ** -0.5),
        'norm_cross': gain(ks[12], (L, D)),
        'norm_mem': gain(ks[13], (L, D)),
        'w_cq': nrm(ks[14], (L, D, D), D ** -0.5),
        'w_ckv': nrm(ks[15], (L, D, 2 * D), D ** -0.5),
        'w_co': nrm(ks[16], (L, D, D), D ** -0.5),
        'norm_ffn': gain(ks[17], (L, D)),
        'w_router_group': nrm(ks[18], (L, D, N_GROUPS), D ** -0.5),
        'b_router_group': nrm(ks[19], (L, N_GROUPS), 0.01),
        'w_router_expert': nrm(ks[20], (L, D, N_EXPERTS), D ** -0.5),
        'b_router_expert': nrm(ks[21], (L, N_EXPERTS), 0.01),
        'w_up': nrm(ks[22], (L, N_EXPERTS, D, 2 * D_EXPERT), D ** -0.5),
        'w_down': nrm(ks[23], (L, N_EXPERTS, D_EXPERT, D), D_EXPERT ** -0.5),
        'norm_final': gain(ks[24], (D,)),
    }


def reference(x_prompt, x_sample, mem_prompt, mem_sample, norm_mix, w_in, w_pool, pool_scale, gate_norm,
              w_spatial, b_spatial, w_out, norm_cross, norm_mem, w_cq, w_ckv, w_co, norm_ffn,
              w_router_group, b_router_group, w_router_expert, b_router_expert, w_up, w_down, norm_final):
    params = {
        'norm_mix': norm_mix, 'w_in': w_in, 'w_pool': w_pool, 'pool_scale': pool_scale,
        'gate_norm': gate_norm, 'w_spatial': w_spatial, 'b_spatial': b_spatial, 'w_out': w_out,
        'norm_cross': norm_cross, 'norm_mem': norm_mem, 'w_cq': w_cq, 'w_ckv': w_ckv, 'w_co': w_co,
        'norm_ffn': norm_ffn, 'w_router_group': w_router_group, 'b_router_group': b_router_group,
        'w_router_expert': w_router_expert, 'b_router_expert': b_router_expert,
        'w_up': w_up, 'w_down': w_down, 'norm_final': norm_final,
    }
    y_prompt = _trunk(x_prompt, mem_prompt, params)
    y_sample = _trunk(x_sample, mem_sample, params)
    return (y_prompt, y_sample)
```

```python
import functools

import jax
import jax.numpy as jnp
from jax import lax
from jax.experimental import pallas as pl
from jax.experimental.pallas import tpu as pltpu

F32 = jnp.float32
BF16 = jnp.bfloat16

D_MODEL = 1024
D_ATTN = 512
D_POOL = 256
D_GATE = 256
D_QKV = 3 * D_ATTN
D_IN = D_QKV + D_POOL + 2 * D_GATE
N_HEADS = 8
HEAD_DIM = 64
RADIUS = 64
DILATIONS = (1, 4, 16)
POOL_WINDOWS = (2, 4, 8, 16)
POOL_HALO = 8
CHUNK = 128
N_GATE_GROUPS = 4
N_MEM = 256
N_CROSS_HEADS = 4
CROSS_HEAD_DIM = 256
N_GROUPS = 4
EXPERTS_PER_GROUP = 4
N_EXPERTS = 16
D_EXPERT = 512
EPS = 1e-6
NEG_INF = -1e30

LANES = 128
Q_SUB = 128
K_WIN = Q_SUB + 2 * RADIUS
VMEM_LIMIT = 56 * 1024 * 1024


def _cparams(*sem):
    return pltpu.CompilerParams(dimension_semantics=sem, vmem_limit_bytes=VMEM_LIMIT)


def _rms(x, g):
    return (x * lax.rsqrt(jnp.mean(x * x, axis=-1, keepdims=True) + EPS)) * g


def _mix_in_kernel(x_ref, g_ref, win_ref, gn_ref, ws_ref, bs_ref, qkv_ref, zb_ref, yc_ref, *, tm):
    h = _rms(x_ref[...], g_ref[...]).astype(BF16)
    qkv_ref[...] = jnp.dot(h, win_ref[:, :D_QKV], preferred_element_type=F32).astype(BF16)
    zb_ref[...] = jnp.dot(h, win_ref[:, D_QKV:D_QKV + D_POOL], preferred_element_type=F32)
    u = jnp.dot(h, win_ref[:, D_QKV + D_POOL:D_QKV + D_POOL + D_GATE], preferred_element_type=F32)
    v = jnp.dot(h, win_ref[:, D_QKV + D_POOL + D_GATE:], preferred_element_type=F32)
    gu = jax.nn.gelu(u)
    vn = _rms(jax.nn.gelu(v), gn_ref[...]).astype(BF16)
    group = lax.broadcasted_iota(jnp.int32, (CHUNK, D_GATE), 1) // (D_GATE // N_GATE_GROUPS)
    for c in range(tm // CHUNK):
        vc = vn[c * CHUNK:(c + 1) * CHUNK]
        f = jnp.zeros((CHUNK, D_GATE), F32)
        for g in range(N_GATE_GROUPS):
            fg = jnp.dot(ws_ref[g], vc, preferred_element_type=F32)
            f = jnp.where(group == g, fg, f)
        yc_ref[c * CHUNK:(c + 1) * CHUNK, :] = (gu[c * CHUNK:(c + 1) * CHUNK] * (f + bs_ref[...])).astype(BF16)


def _mix_in(x2d, norm_g, w_in, gate_norm, w_sp, b_full, tm):
    n = x2d.shape[0]
    const = lambda i: (0, 0)
    return pl.pallas_call(
        functools.partial(_mix_in_kernel, tm=tm),
        grid=(n // tm,),
        in_specs=[
            pl.BlockSpec((tm, D_MODEL), lambda i: (i, 0)),
            pl.BlockSpec((1, D_MODEL), const),
            pl.BlockSpec((D_MODEL, D_IN), const),
            pl.BlockSpec((1, D_GATE), const),
            pl.BlockSpec((N_GATE_GROUPS, CHUNK, CHUNK), lambda i: (0, 0, 0)),
            pl.BlockSpec((CHUNK, D_GATE), const),
        ],
        out_specs=[
            pl.BlockSpec((tm, D_QKV), lambda i: (i, 0)),
            pl.BlockSpec((tm, D_POOL), lambda i: (i, 0)),
            pl.BlockSpec((tm, D_GATE), lambda i: (i, 0)),
        ],
        out_shape=[
            jax.ShapeDtypeStruct((n, D_QKV), BF16),
            jax.ShapeDtypeStruct((n, D_POOL), F32),
            jax.ShapeDtypeStruct((n, D_GATE), BF16),
        ],
        compiler_params=_cparams("parallel"),
        name="mix_in",
    )(x2d, norm_g, w_in, gate_norm, w_sp, b_full)


def _attn_kernel(q_ref, kp_ref, kc_ref, kn_ref, vp_ref, vc_ref, vn_ref, bias_ref, o_ref, lse_ref,
                 kw_ref, vw_ref, *, lq, seq):
    kw_ref[0:RADIUS] = kp_ref[0]
    kw_ref[RADIUS:RADIUS + lq] = kc_ref[0]
    kw_ref[RADIUS + lq:] = kn_ref[0]
    vw_ref[0:RADIUS] = vp_ref[0]
    vw_ref[RADIUS:RADIUS + lq] = vc_ref[0]
    vw_ref[RADIUS + lq:] = vn_ref[0]
    blk = pl.program_id(2)
    low_half = lax.broadcasted_iota(jnp.int32, (Q_SUB, LANES), 1) < HEAD_DIM
    lane = lax.broadcasted_iota(jnp.int32, (Q_SUB, LANES), 1)
    kcol = lax.broadcasted_iota(jnp.int32, (1, K_WIN), 1)

    def sub_block(j, carry):
        r0 = pl.multiple_of(j * Q_SUB, Q_SUB)
        kpos = kcol + (blk * lq + j * Q_SUB - RADIUS)
        cmask = jnp.where((kpos >= 0) & (kpos < seq), 0.0, NEG_INF).astype(F32)
        lse_tile = jnp.zeros((Q_SUB, LANES), F32)
        for p in range(N_HEADS // 2):
            cols = slice(p * LANES, (p + 1) * LANES)
            q = q_ref[0, pl.ds(r0, Q_SUB), cols]
            kw = kw_ref[pl.ds(r0, K_WIN), cols]
            vw = vw_ref[pl.ds(r0, K_WIN), cols]
            outs = []
            for hh in range(2):
                qm = jnp.where(low_half == (hh == 0), q, jnp.zeros_like(q))
                s = lax.dot_general(qm, kw, (((1,), (1,)), ((), ())), preferred_element_type=F32)
                s = s * (HEAD_DIM ** -0.5) + bias_ref[2 * p + hh] + cmask
                m = jnp.max(s, axis=-1, keepdims=True)
                pe = jnp.exp(s - m)
                den = jnp.sum(pe, axis=-1, keepdims=True)
                outs.append(jnp.dot(pe.astype(BF16), vw, preferred_element_type=F32) / den)
                lse_tile = jnp.where(lane == 2 * p + hh, m + jnp.log(den), lse_tile)
            o_ref[0, pl.ds(r0, Q_SUB), cols] = jnp.where(low_half, outs[0], outs[1]).astype(BF16)
        lse_ref[0, pl.ds(r0, Q_SUB), :] = lse_tile
        return carry

    lax.fori_loop(0, lq // Q_SUB, sub_block, 0)


def _attn_pattern(qkv, bias, batch, seq_full, dil):
    L = seq_full // dil
    lq = min(512, L)
    nblk = L // lq
    hb = lq // RADIUS
    last_halo = L // RADIUS - 1
    qkv_v = qkv.reshape(batch, L, dil * D_QKV)
    nq = D_QKV // D_ATTN
    prev = lambda i: jnp.maximum(i * hb - 1, 0)
    nxt = lambda i: jnp.minimum((i + 1) * hb, last_halo)
    o, lse = pl.pallas_call(
        functools.partial(_attn_kernel, lq=lq, seq=L),
        grid=(batch, dil, nblk),
        in_specs=[
            pl.BlockSpec((1, lq, D_ATTN), lambda b, c, i: (b, i, c * nq)),
            pl.BlockSpec((1, RADIUS, D_ATTN), lambda b, c, i: (b, prev(i), c * nq + 1)),
            pl.BlockSpec((1, lq, D_ATTN), lambda b, c, i: (b, i, c * nq + 1)),
            pl.BlockSpec((1, RADIUS, D_ATTN), lambda b, c, i: (b, nxt(i), c * nq + 1)),
            pl.BlockSpec((1, RADIUS, D_ATTN), lambda b, c, i: (b, prev(i), c * nq + 2)),
            pl.BlockSpec((1, lq, D_ATTN), lambda b, c, i: (b, i, c * nq + 2)),
            pl.BlockSpec((1, RADIUS, D_ATTN), lambda b, c, i: (b, nxt(i), c * nq + 2)),
            pl.BlockSpec((N_HEADS, Q_SUB, K_WIN), lambda b, c, i: (0, 0, 0)),
        ],
        out_specs=[
            pl.BlockSpec((1, lq, D_ATTN), lambda b, c, i: (b, i, c)),
            pl.BlockSpec((1, lq, LANES), lambda b, c, i: (b, i, c)),
        ],
        out_shape=[
            jax.ShapeDtypeStruct((batch, L, dil * D_ATTN), BF16),
            jax.ShapeDtypeStruct((batch, L, dil * LANES), F32),
        ],
        scratch_shapes=[
            pltpu.VMEM((lq + 2 * RADIUS, D_ATTN), BF16),
            pltpu.VMEM((lq + 2 * RADIUS, D_ATTN), BF16),
        ],
        compiler_params=_cparams("parallel", "parallel", "parallel"),
        name=f"attn_d{dil}",
    )(qkv_v, qkv_v, qkv_v, qkv_v, qkv_v, qkv_v, qkv_v, bias)
    n = batch * seq_full
    return o.reshape(n, D_ATTN), lse.reshape(n, LANES)


def _attn_bias(dil):
    slopes = jnp.exp2(-8.0 * jnp.arange(1, N_HEADS + 1, dtype=F32) / N_HEADS)
    rel = jnp.arange(K_WIN)[None, :] - RADIUS - jnp.arange(Q_SUB)[:, None]
    dist = (jnp.abs(rel) * dil).astype(F32)
    bias = -(slopes[:, None, None] * dist[None])
    return jnp.where((jnp.abs(rel) <= RADIUS)[None], bias, NEG_INF).astype(F32)


def _mix_out_kernel(x_ref, o1_ref, o2_ref, o3_ref, l1_ref, l2_ref, l3_ref, zp_ref, zc_ref, zn_ref, yc_ref,
                    ex_ref, wp_ref, ps_ref, wo_ref, out_ref, ext_ref, *, tm, seq):
    i = pl.program_id(0)
    l1, l2, l3 = l1_ref[...], l2_ref[...], l3_ref[...]
    mx = jnp.maximum(jnp.maximum(l1, l2), l3)
    e1, e2, e3 = jnp.exp(l1 - mx), jnp.exp(l2 - mx), jnp.exp(l3 - mx)
    den = e1 + e2 + e3
    ya = jnp.zeros((tm, D_ATTN), F32)
    for e, o_ref in ((e1, o1_ref), (e2, o2_ref), (e3, o3_ref)):
        a = jnp.dot((e / den).astype(BF16), ex_ref[...], preferred_element_type=F32)
        ya = ya + a * o_ref[...].astype(F32)

    pos0 = (i * tm) % seq
    zero_halo = jnp.zeros((POOL_HALO, D_POOL), F32)
    ext_ref[0:POOL_HALO] = jnp.where(pos0 == 0, zero_halo, zp_ref[...])
    ext_ref[POOL_HALO:POOL_HALO + tm] = zc_ref[...]
    ext_ref[POOL_HALO + tm:] = jnp.where(pos0 + tm == seq, zero_halo, zn_ref[...])
    pos = pos0 + lax.broadcasted_iota(jnp.int32, (tm, 1), 0)
    lane_group = lax.broadcasted_iota(jnp.int32, (tm, LANES), 1) // (D_POOL // len(POOL_WINDOWS))

    def shifted(off, slab):
        return ext_ref[POOL_HALO + off:POOL_HALO + off + tm, slab * LANES:(slab + 1) * LANES]

    def window_mean(total, w):
        cnt = jnp.minimum(pos + w // 2, seq) - jnp.maximum(pos - w // 2, 0)
        return total / cnt.astype(F32)

    pooled = []
    for slab in range(2):
        w_small, w_big = POOL_WINDOWS[2 * slab], POOL_WINDOWS[2 * slab + 1]
        s_small = shifted(-(w_small // 2), slab)
        for off in range(-(w_small // 2) + 1, w_small // 2):
            s_small = s_small + shifted(off, slab)
        s_big = s_small
        for off in list(range(-(w_big // 2), -(w_small // 2))) + list(range(w_small // 2, w_big // 2)):
            s_big = s_big + shifted(off, slab)
        mean = jnp.where(lane_group == 0, window_mean(s_small, w_small), window_mean(s_big, w_big))
        pooled.append(mean - shifted(0, slab))
    pm = jnp.concatenate(pooled, axis=-1).astype(BF16)
    yb = jnp.dot(pm, wp_ref[...], preferred_element_type=F32) * ps_ref[...]

    acc = jnp.dot(ya.astype(BF16), wo_ref[0:D_ATTN, :], preferred_element_type=F32)
    acc = acc + jnp.dot(yb.astype(BF16), wo_ref[D_ATTN:D_ATTN + D_POOL, :], preferred_element_type=F32)
    acc = acc + jnp.dot(yc_ref[...], wo_ref[D_ATTN + D_POOL:, :], preferred_element_type=F32)
    out_ref[...] = x_ref[...] + acc


def _mix_out(x2d, outs, lses, zb, yc, expand, w_pool_bd, pool_scale, w_out, tm, seq):
    n = x2d.shape[0]
    const = lambda i: (0, 0)
    row = lambda i: (i, 0)
    hb = tm // POOL_HALO
    last = n // POOL_HALO - 1
    return pl.pallas_call(
        functools.partial(_mix_out_kernel, tm=tm, seq=seq),
        grid=(n // tm,),
        in_specs=[
            pl.BlockSpec((tm, D_MODEL), row),
            pl.BlockSpec((tm, D_ATTN), row), pl.BlockSpec((tm, D_ATTN), row), pl.BlockSpec((tm, D_ATTN), row),
            pl.BlockSpec((tm, LANES), row), pl.BlockSpec((tm, LANES), row), pl.BlockSpec((tm, LANES), row),
            pl.BlockSpec((POOL_HALO, D_POOL), lambda i: (jnp.maximum(i * hb - 1, 0), 0)),
            pl.BlockSpec((tm, D_POOL), row),
            pl.BlockSpec((POOL_HALO, D_POOL), lambda i: (jnp.minimum((i + 1) * hb, last), 0)),
            pl.BlockSpec((tm, D_GATE), row),
            pl.BlockSpec((LANES, D_ATTN), const),
            pl.BlockSpec((D_POOL, D_POOL), const),
            pl.BlockSpec((1, D_POOL), const),
            pl.BlockSpec((D_MODEL, D_MODEL), const),
        ],
        out_specs=pl.BlockSpec((tm, D_MODEL), row),
        out_shape=jax.ShapeDtypeStruct((n, D_MODEL), F32),
        scratch_shapes=[pltpu.VMEM((tm + 2 * POOL_HALO, D_POOL), F32)],
        compiler_params=_cparams("parallel"),
        name="mix_out",
    )(x2d, outs[0], outs[1], outs[2], lses[0], lses[1], lses[2], zb, zb, zb, yc,
      expand, w_pool_bd, pool_scale, w_out)


def _mem_kv_kernel(m_ref, g_ref, w_ref, kv_ref):
    mn = _rms(m_ref[...], g_ref[...]).astype(BF16)
    kv_ref[...] = jnp.dot(mn, w_ref[...], preferred_element_type=F32).astype(BF16)


def _mem_kv(mem2d, norm_g, w_ckv):
    n = mem2d.shape[0]
    tm = N_MEM
    return pl.pallas_call(
        _mem_kv_kernel,
        grid=(n // tm,),
        in_specs=[
            pl.BlockSpec((tm, D_MODEL), lambda i: (i, 0)),
            pl.BlockSpec((1, D_MODEL), lambda i: (0, 0)),
            pl.BlockSpec((D_MODEL, 2 * D_MODEL), lambda i: (0, 0)),
        ],
        out_specs=pl.BlockSpec((tm, 2 * D_MODEL), lambda i: (i, 0)),
        out_shape=jax.ShapeDtypeStruct((n, 2 * D_MODEL), BF16),
        compiler_params=_cparams("parallel"),
        name="mem_kv",
    )(mem2d, norm_g, w_ckv)


def _cross_kernel(x_ref, g_ref, wq_ref, kv_ref, wo_ref, out_ref):
    x = x_ref[...]
    hc = _rms(x, g_ref[...]).astype(BF16)
    q = jnp.dot(hc, wq_ref[...], preferred_element_type=F32).astype(BF16)
    heads = []
    for h in range(N_CROSS_HEADS):
        cols = slice(h * CROSS_HEAD_DIM, (h + 1) * CROSS_HEAD_DIM)
        k = kv_ref[0, :, cols]
        v = kv_ref[0, :, D_MODEL + h * CROSS_HEAD_DIM:D_MODEL + (h + 1) * CROSS_HEAD_DIM]
        s = lax.dot_general(q[:, cols], k, (((1,), (1,)), ((), ())), preferred_element_type=F32)
        s = s * (CROSS_HEAD_DIM ** -0.5)
        m = jnp.max(s, axis=-1, keepdims=True)
        pe = jnp.exp(s - m)
        den = jnp.sum(pe, axis=-1, keepdims=True)
        heads.append((jnp.dot(pe.astype(BF16), v, preferred_element_type=F32) / den).astype(BF16))
    o = jnp.concatenate(heads, axis=-1)
    out_ref[...] = x + jnp.dot(o, wo_ref[...], preferred_element_type=F32)


def _cross(x2d, norm_g, w_cq, kv, w_co, tm, seq):
    n = x2d.shape[0]
    const = lambda i: (0, 0)
    per_seq = seq // tm
    return pl.pallas_call(
        _cross_kernel,
        grid=(n // tm,),
        in_specs=[
            pl.BlockSpec((tm, D_MODEL), lambda i: (i, 0)),
            pl.BlockSpec((1, D_MODEL), const),
            pl.BlockSpec((D_MODEL, D_MODEL), const),
            pl.BlockSpec((1, N_MEM, 2 * D_MODEL), lambda i: (i // per_seq, 0, 0)),
            pl.BlockSpec((D_MODEL, D_MODEL), const),
        ],
        out_specs=pl.BlockSpec((tm, D_MODEL), lambda i: (i, 0)),
        out_shape=jax.ShapeDtypeStruct((n, D_MODEL), F32),
        compiler_params=_cparams("parallel"),
        name="cross",
    )(x2d, norm_g, w_cq, kv, w_co)


def _route(hf, wr_hi_ref, wr_lo_ref, br_ref):
    tm = hf.shape[0]
    hi = hf.astype(BF16)
    lo = (hf - hi.astype(F32)).astype(BF16)
    lg = (jnp.dot(hi, wr_hi_ref[...], preferred_element_type=F32)
          + jnp.dot(lo, wr_hi_ref[...], preferred_element_type=F32)
          + jnp.dot(hi, wr_lo_ref[...], preferred_element_type=F32)) + br_ref[...]
    lane = lax.broadcasted_iota(jnp.int32, (tm, LANES), 1)
    lane_f = lane.astype(F32)
    big = jnp.float32(LANES)
    is_group = lane < N_GROUPS
    g_logit = jnp.where(is_group, lg, NEG_INF)
    g_max = jnp.max(g_logit, axis=-1, keepdims=True)
    g_idx = jnp.min(jnp.where(is_group & (g_logit == g_max), lane_f, big), axis=-1, keepdims=True)
    g_w = 1.0 / jnp.sum(jnp.where(is_group, jnp.exp(g_logit - g_max), 0.0), axis=-1, keepdims=True)
    first = N_GROUPS + EXPERTS_PER_GROUP * g_idx
    sel = (lane_f >= first) & (lane_f < first + EXPERTS_PER_GROUP)
    v1 = jnp.max(jnp.where(sel, lg, NEG_INF), axis=-1, keepdims=True)
    i1 = jnp.min(jnp.where(sel & (lg == v1), lane_f, big), axis=-1, keepdims=True)
    sel2 = sel & (lane_f != i1)
    v2 = jnp.max(jnp.where(sel2, lg, NEG_INF), axis=-1, keepdims=True)
    i2 = jnp.min(jnp.where(sel2 & (lg == v2), lane_f, big), axis=-1, keepdims=True)
    e2 = jnp.exp(v2 - v1)
    w1 = g_w / (1.0 + e2)
    w2 = g_w * e2 / (1.0 + e2)
    return jnp.where(lane_f == i1, w1, 0.0) + jnp.where(lane_f == i2, w2, 0.0)


def _moe_dense_kernel(x_ref, g_ref, wrh_ref, wrl_ref, br_ref, wup_ref, wdn_ref, gf_ref, out_ref,
                      hf_ref, gates_ref, *, tm, final):
    e = pl.program_id(1)

    @pl.when(e == 0)
    def _():
        x = x_ref[...]
        hf = _rms(x, g_ref[...])
        hf_ref[...] = hf.astype(BF16)
        gates_ref[...] = _route(hf, wrh_ref, wrl_ref, br_ref)
        out_ref[...] = x

    gu = jnp.dot(hf_ref[...], wup_ref[0], preferred_element_type=F32)
    gate_half, up_half = gu[:, :D_EXPERT], gu[:, D_EXPERT:]
    a = (gate_half * jax.nn.sigmoid(gate_half)) * up_half
    y = jnp.dot(a.astype(BF16), wdn_ref[0], preferred_element_type=F32)
    lane = lax.broadcasted_iota(jnp.int32, (tm, LANES), 1)
    w = jnp.sum(jnp.where(lane == N_GROUPS + e, gates_ref[...], 0.0), axis=-1, keepdims=True)
    out_ref[...] += w * y

    if final:
        @pl.when(e == N_EXPERTS - 1)
        def _():
            out_ref[...] = _rms(out_ref[...], gf_ref[...])


def _moe_dense(x2d, norm_g, wr_hi, wr_lo, b_r, w_up, w_down, norm_final, tm, final):
    n = x2d.shape[0]
    const = lambda i, e: (0, 0)
    return pl.pallas_call(
        functools.partial(_moe_dense_kernel, tm=tm, final=final),
        grid=(n // tm, N_EXPERTS),
        in_specs=[
            pl.BlockSpec((tm, D_MODEL), lambda i, e: (i, 0)),
            pl.BlockSpec((1, D_MODEL), const),
            pl.BlockSpec((D_MODEL, LANES), const),
            pl.BlockSpec((D_MODEL, LANES), const),
            pl.BlockSpec((1, LANES), const),
            pl.BlockSpec((1, D_MODEL, 2 * D_EXPERT), lambda i, e: (e, 0, 0)),
            pl.BlockSpec((1, D_EXPERT, D_MODEL), lambda i, e: (e, 0, 0)),
            pl.BlockSpec((1, D_MODEL), const),
        ],
        out_specs=pl.BlockSpec((tm, D_MODEL), lambda i, e: (i, 0)),
        out_shape=jax.ShapeDtypeStruct((n, D_MODEL), F32),
        scratch_shapes=[pltpu.VMEM((tm, D_MODEL), BF16), pltpu.VMEM((tm, LANES), F32)],
        compiler_params=_cparams("parallel", "arbitrary"),
        name="moe_dense",
    )(x2d, norm_g, wr_hi, wr_lo, b_r, w_up, w_down, norm_final)


def _prep_layer(p, l):
    b_full = jnp.repeat(jnp.transpose(p["b_spatial"][l]), D_GATE // N_GATE_GROUPS, axis=1)
    wp = p["w_pool"][l]
    gsz = D_POOL // len(POOL_WINDOWS)
    w_pool_bd = jnp.zeros((D_POOL, D_POOL), F32)
    for g in range(len(POOL_WINDOWS)):
        w_pool_bd = w_pool_bd.at[g * gsz:(g + 1) * gsz, g * gsz:(g + 1) * gsz].set(wp[g])
    w_r = jnp.zeros((D_MODEL, LANES), F32)
    w_r = w_r.at[:, :N_GROUPS].set(p["w_router_group"][l])
    w_r = w_r.at[:, N_GROUPS:N_GROUPS + N_EXPERTS].set(p["w_router_expert"][l])
    wr_hi = w_r.astype(BF16)
    wr_lo = (w_r - wr_hi.astype(F32)).astype(BF16)
    b_r = jnp.zeros((1, LANES), F32)
    b_r = b_r.at[0, :N_GROUPS].set(p["b_router_group"][l])
    b_r = b_r.at[0, N_GROUPS:N_GROUPS + N_EXPERTS].set(p["b_router_expert"][l])
    return dict(
        norm_mix=p["norm_mix"][l][None], w_in=p["w_in"][l].astype(BF16),
        gate_norm=p["gate_norm"][l][None], w_sp=p["w_spatial"][l].astype(BF16), b_full=b_full,
        w_pool_bd=w_pool_bd.astype(BF16), pool_scale=p["pool_scale"][l][None], w_out=p["w_out"][l].astype(BF16),
        norm_cross=p["norm_cross"][l][None], norm_mem=p["norm_mem"][l][None],
        w_cq=p["w_cq"][l].astype(BF16), w_ckv=p["w_ckv"][l].astype(BF16), w_co=p["w_co"][l].astype(BF16),
        norm_ffn=p["norm_ffn"][l][None], wr_hi=wr_hi, wr_lo=wr_lo, b_r=b_r,
        w_up=p["w_up"][l].astype(BF16), w_down=p["w_down"][l].astype(BF16),
    )


def _prepare(p):
    depth = p["norm_mix"].shape[0]
    head_of_lane = jnp.arange(D_ATTN) // HEAD_DIM
    return dict(
        layers=[_prep_layer(p, l) for l in range(depth)],
        biases=[_attn_bias(d) for d in DILATIONS],
        expand=(jnp.arange(LANES)[:, None] == head_of_lane[None, :]).astype(BF16),
        norm_final=p["norm_final"][None],
    )


def _trunk(x, mem, prep):
    batch, seq, _ = x.shape
    n = batch * seq
    x2d = x.reshape(n, D_MODEL)
    mem2d = mem.reshape(batch * N_MEM, D_MODEL)
    layers = prep["layers"]
    tm = 512
    for li, lp in enumerate(layers):
        qkv, zb, yc = _mix_in(x2d, lp["norm_mix"], lp["w_in"], lp["gate_norm"], lp["w_sp"], lp["b_full"], tm)
        outs, lses = [], []
        for dil, bias in zip(DILATIONS, prep["biases"]):
            o, lse = _attn_pattern(qkv, bias, batch, seq, dil)
            outs.append(o)
            lses.append(lse)
        x2d = _mix_out(x2d, outs, lses, zb, yc, prep["expand"], lp["w_pool_bd"], lp["pool_scale"], lp["w_out"],
                       tm, seq)
        kv = _mem_kv(mem2d, lp["norm_mem"], lp["w_ckv"]).reshape(batch, N_MEM, 2 * D_MODEL)
        x2d = _cross(x2d, lp["norm_cross"], lp["w_cq"], kv, lp["w_co"], tm, seq)
        x2d = _moe_dense(x2d, lp["norm_ffn"], lp["wr_hi"], lp["wr_lo"], lp["b_r"], lp["w_up"], lp["w_down"],
                         prep["norm_final"], 1024, li == len(layers) - 1)
    return x2d.reshape(batch, seq, D_MODEL)


def kernel(x_prompt, x_sample, mem_prompt, mem_sample, norm_mix, w_in, w_pool, pool_scale, gate_norm, w_spatial, b_spatial, w_out, norm_cross, norm_mem, w_cq, w_ckv, w_co, norm_ffn, w_router_group, b_router_group, w_router_expert, b_router_expert, w_up, w_down, norm_final):
    prep = _prepare(dict(
        norm_mix=norm_mix, w_in=w_in, w_pool=w_pool, pool_scale=pool_scale, gate_norm=gate_norm,
        w_spatial=w_spatial, b_spatial=b_spatial, w_out=w_out, norm_cross=norm_cross, norm_mem=norm_mem,
        w_cq=w_cq, w_ckv=w_ckv, w_co=w_co, norm_ffn=norm_ffn, w_router_group=w_router_group,
        b_router_group=b_router_group, w_router_expert=w_router_expert, b_router_expert=b_router_expert,
        w_up=w_up, w_down=w_down, norm_final=norm_final))
    return (_trunk(x_prompt, mem_prompt, prep), _trunk(x_sample, mem_sample, prep))
```

```python
import functools

import jax
import jax.numpy as jnp
from jax import lax
from jax.experimental import pallas as pl
from jax.experimental.pallas import tpu as pltpu

F32 = jnp.float32
BF16 = jnp.bfloat16

D_MODEL = 1024
D_ATTN = 512
D_POOL = 256
D_GATE = 256
D_QKV = 3 * D_ATTN
D_IN = D_QKV + D_POOL + 2 * D_GATE
N_HEADS = 8
HEAD_DIM = 64
RADIUS = 64
DILATIONS = (1, 4, 16)
POOL_WINDOWS = (2, 4, 8, 16)
POOL_HALO = 8
CHUNK = 128
N_GATE_GROUPS = 4
N_MEM = 256
N_CROSS_HEADS = 4
CROSS_HEAD_DIM = 256
N_GROUPS = 4
EXPERTS_PER_GROUP = 4
N_EXPERTS = 16
D_EXPERT = 512
PAIRS_PER_GROUP = 6
N_BUCKETS = N_GROUPS * PAIRS_PER_GROUP
EPS = 1e-6
NEG_INF = -1e30

LANES = 128
Q_SUB = 128
K_WIN = Q_SUB + 2 * RADIUS
VMEM_LIMIT = 56 * 1024 * 1024
TOKEN_TILE = 512
MOE_TILE = 256
D_AUG = D_MODEL + LANES
ROUTE_BUCKET, ROUTE_RANK, ROUTE_W_LO, ROUTE_W_HI = 0, 1, 2, 3


def _cparams(*sem):
    return pltpu.CompilerParams(dimension_semantics=sem, vmem_limit_bytes=VMEM_LIMIT)


def _rms(x, g):
    return (x * lax.rsqrt(jnp.mean(x * x, axis=-1, keepdims=True) + EPS)) * g


def _mix_in_kernel(x_ref, g_ref, win_ref, gn_ref, ws_ref, bs_ref, qkv_ref, zb_ref, yc_ref, *, tm):
    h = _rms(x_ref[...], g_ref[...]).astype(BF16)
    qkv_ref[...] = jnp.dot(h, win_ref[:, :D_QKV], preferred_element_type=F32).astype(BF16)
    zb_ref[...] = jnp.dot(h, win_ref[:, D_QKV:D_QKV + D_POOL], preferred_element_type=F32)
    u = jnp.dot(h, win_ref[:, D_QKV + D_POOL:D_QKV + D_POOL + D_GATE], preferred_element_type=F32)
    v = jnp.dot(h, win_ref[:, D_QKV + D_POOL + D_GATE:], preferred_element_type=F32)
    gu = jax.nn.gelu(u)
    vn = _rms(jax.nn.gelu(v), gn_ref[...]).astype(BF16)
    group = lax.broadcasted_iota(jnp.int32, (CHUNK, D_GATE), 1) // (D_GATE // N_GATE_GROUPS)
    for c in range(tm // CHUNK):
        vc = vn[c * CHUNK:(c + 1) * CHUNK]
        f = jnp.zeros((CHUNK, D_GATE), F32)
        for g in range(N_GATE_GROUPS):
            fg = jnp.dot(ws_ref[g], vc, preferred_element_type=F32)
            f = jnp.where(group == g, fg, f)
        yc_ref[c * CHUNK:(c + 1) * CHUNK, :] = (gu[c * CHUNK:(c + 1) * CHUNK] * (f + bs_ref[...])).astype(BF16)


def _mix_in(x2d, norm_g, w_in, gate_norm, w_sp, b_full, tm):
    n = x2d.shape[0]
    const = lambda i: (0, 0)
    return pl.pallas_call(
        functools.partial(_mix_in_kernel, tm=tm),
        grid=(n // tm,),
        in_specs=[
            pl.BlockSpec((tm, D_MODEL), lambda i: (i, 0)),
            pl.BlockSpec((1, D_MODEL), const),
            pl.BlockSpec((D_MODEL, D_IN), const),
            pl.BlockSpec((1, D_GATE), const),
            pl.BlockSpec((N_GATE_GROUPS, CHUNK, CHUNK), lambda i: (0, 0, 0)),
            pl.BlockSpec((CHUNK, D_GATE), const),
        ],
        out_specs=[
            pl.BlockSpec((tm, D_QKV), lambda i: (i, 0)),
            pl.BlockSpec((tm, D_POOL), lambda i: (i, 0)),
            pl.BlockSpec((tm, D_GATE), lambda i: (i, 0)),
        ],
        out_shape=[
            jax.ShapeDtypeStruct((n, D_QKV), BF16),
            jax.ShapeDtypeStruct((n, D_POOL), F32),
            jax.ShapeDtypeStruct((n, D_GATE), BF16),
        ],
        compiler_params=_cparams("parallel"),
        name="mix_in",
    )(x2d, norm_g, w_in, gate_norm, w_sp, b_full)


def _attn_kernel(q_ref, kp_ref, kc_ref, kn_ref, vp_ref, vc_ref, vn_ref, bias_ref, o_ref, lse_ref,
                 kw_ref, vw_ref, *, lq, seq):
    kw_ref[0:RADIUS] = kp_ref[0]
    kw_ref[RADIUS:RADIUS + lq] = kc_ref[0]
    kw_ref[RADIUS + lq:] = kn_ref[0]
    vw_ref[0:RADIUS] = vp_ref[0]
    vw_ref[RADIUS:RADIUS + lq] = vc_ref[0]
    vw_ref[RADIUS + lq:] = vn_ref[0]
    blk = pl.program_id(2)
    low_half = lax.broadcasted_iota(jnp.int32, (Q_SUB, LANES), 1) < HEAD_DIM
    lane = lax.broadcasted_iota(jnp.int32, (Q_SUB, LANES), 1)
    kcol = lax.broadcasted_iota(jnp.int32, (1, K_WIN), 1)

    def sub_block(j, carry):
        r0 = pl.multiple_of(j * Q_SUB, Q_SUB)
        kpos = kcol + (blk * lq + j * Q_SUB - RADIUS)
        cmask = jnp.where((kpos >= 0) & (kpos < seq), 0.0, NEG_INF).astype(F32)
        lse_tile = jnp.zeros((Q_SUB, LANES), F32)
        for p in range(N_HEADS // 2):
            cols = slice(p * LANES, (p + 1) * LANES)
            q = q_ref[0, pl.ds(r0, Q_SUB), cols]
            kw = kw_ref[pl.ds(r0, K_WIN), cols]
            vw = vw_ref[pl.ds(r0, K_WIN), cols]
            outs = []
            for hh in range(2):
                qm = jnp.where(low_half == (hh == 0), q, jnp.zeros_like(q))
                s = lax.dot_general(qm, kw, (((1,), (1,)), ((), ())), preferred_element_type=F32)
                s = s * (HEAD_DIM ** -0.5) + bias_ref[2 * p + hh] + cmask
                m = jnp.max(s, axis=-1, keepdims=True)
                pe = jnp.exp(s - m)
                den = jnp.sum(pe, axis=-1, keepdims=True)
                outs.append(jnp.dot(pe.astype(BF16), vw, preferred_element_type=F32) / den)
                lse_tile = jnp.where(lane == 2 * p + hh, m + jnp.log(den), lse_tile)
            o_ref[0, pl.ds(r0, Q_SUB), cols] = jnp.where(low_half, outs[0], outs[1]).astype(BF16)
        lse_ref[0, pl.ds(r0, Q_SUB), :] = lse_tile
        return carry

    lax.fori_loop(0, lq // Q_SUB, sub_block, 0)


def _attn_pattern(qkv, bias, batch, seq_full, dil):
    L = seq_full // dil
    lq = min(512, L)
    nblk = L // lq
    hb = lq // RADIUS
    last_halo = L // RADIUS - 1
    qkv_v = qkv.reshape(batch, L, dil * D_QKV)
    nq = D_QKV // D_ATTN
    prev = lambda i: jnp.maximum(i * hb - 1, 0)
    nxt = lambda i: jnp.minimum((i + 1) * hb, last_halo)
    o, lse = pl.pallas_call(
        functools.partial(_attn_kernel, lq=lq, seq=L),
        grid=(batch, dil, nblk),
        in_specs=[
            pl.BlockSpec((1, lq, D_ATTN), lambda b, c, i: (b, i, c * nq)),
            pl.BlockSpec((1, RADIUS, D_ATTN), lambda b, c, i: (b, prev(i), c * nq + 1)),
            pl.BlockSpec((1, lq, D_ATTN), lambda b, c, i: (b, i, c * nq + 1)),
            pl.BlockSpec((1, RADIUS, D_ATTN), lambda b, c, i: (b, nxt(i), c * nq + 1)),
            pl.BlockSpec((1, RADIUS, D_ATTN), lambda b, c, i: (b, prev(i), c * nq + 2)),
            pl.BlockSpec((1, lq, D_ATTN), lambda b, c, i: (b, i, c * nq + 2)),
            pl.BlockSpec((1, RADIUS, D_ATTN), lambda b, c, i: (b, nxt(i), c * nq + 2)),
            pl.BlockSpec((N_HEADS, Q_SUB, K_WIN), lambda b, c, i: (0, 0, 0)),
        ],
        out_specs=[
            pl.BlockSpec((1, lq, D_ATTN), lambda b, c, i: (b, i, c)),
            pl.BlockSpec((1, lq, LANES), lambda b, c, i: (b, i, c)),
        ],
        out_shape=[
            jax.ShapeDtypeStruct((batch, L, dil * D_ATTN), BF16),
            jax.ShapeDtypeStruct((batch, L, dil * LANES), F32),
        ],
        scratch_shapes=[
            pltpu.VMEM((lq + 2 * RADIUS, D_ATTN), BF16),
            pltpu.VMEM((lq + 2 * RADIUS, D_ATTN), BF16),
        ],
        compiler_params=_cparams("parallel", "parallel", "parallel"),
        name=f"attn_d{dil}",
    )(qkv_v, qkv_v, qkv_v, qkv_v, qkv_v, qkv_v, qkv_v, bias)
    n = batch * seq_full
    return o.reshape(n, D_ATTN), lse.reshape(n, LANES)


def _attn_bias(dil):
    slopes = jnp.exp2(-8.0 * jnp.arange(1, N_HEADS + 1, dtype=F32) / N_HEADS)
    rel = jnp.arange(K_WIN)[None, :] - RADIUS - jnp.arange(Q_SUB)[:, None]
    dist = (jnp.abs(rel) * dil).astype(F32)
    bias = -(slopes[:, None, None] * dist[None])
    return jnp.where((jnp.abs(rel) <= RADIUS)[None], bias, NEG_INF).astype(F32)


def _mix_out_kernel(x_ref, o1_ref, o2_ref, o3_ref, l1_ref, l2_ref, l3_ref, zp_ref, zc_ref, zn_ref, yc_ref,
                    ex_ref, wp_ref, ps_ref, wo_ref, out_ref, ext_ref, *, tm, seq):
    i = pl.program_id(0)
    l1, l2, l3 = l1_ref[...], l2_ref[...], l3_ref[...]
    mx = jnp.maximum(jnp.maximum(l1, l2), l3)
    e1, e2, e3 = jnp.exp(l1 - mx), jnp.exp(l2 - mx), jnp.exp(l3 - mx)
    den = e1 + e2 + e3
    ya = jnp.zeros((tm, D_ATTN), F32)
    for e, o_ref in ((e1, o1_ref), (e2, o2_ref), (e3, o3_ref)):
        a = jnp.dot((e / den).astype(BF16), ex_ref[...], preferred_element_type=F32)
        ya = ya + a * o_ref[...].astype(F32)

    pos0 = (i * tm) % seq
    zero_halo = jnp.zeros((POOL_HALO, D_POOL), F32)
    ext_ref[0:POOL_HALO] = jnp.where(pos0 == 0, zero_halo, zp_ref[...])
    ext_ref[POOL_HALO:POOL_HALO + tm] = zc_ref[...]
    ext_ref[POOL_HALO + tm:] = jnp.where(pos0 + tm == seq, zero_halo, zn_ref[...])
    pos = pos0 + lax.broadcasted_iota(jnp.int32, (tm, 1), 0)
    lane_group = lax.broadcasted_iota(jnp.int32, (tm, LANES), 1) // (D_POOL // len(POOL_WINDOWS))

    def shifted(off, slab):
        return ext_ref[POOL_HALO + off:POOL_HALO + off + tm, slab * LANES:(slab + 1) * LANES]

    def window_mean(total, w):
        cnt = jnp.minimum(pos + w // 2, seq) - jnp.maximum(pos - w // 2, 0)
        return total / cnt.astype(F32)

    pooled = []
    for slab in range(2):
        w_small, w_big = POOL_WINDOWS[2 * slab], POOL_WINDOWS[2 * slab + 1]
        s_small = shifted(-(w_small // 2), slab)
        for off in range(-(w_small // 2) + 1, w_small // 2):
            s_small = s_small + shifted(off, slab)
        s_big = s_small
        for off in list(range(-(w_big // 2), -(w_small // 2))) + list(range(w_small // 2, w_big // 2)):
            s_big = s_big + shifted(off, slab)
        mean = jnp.where(lane_group == 0, window_mean(s_small, w_small), window_mean(s_big, w_big))
        pooled.append(mean - shifted(0, slab))
    pm = jnp.concatenate(pooled, axis=-1).astype(BF16)
    yb = jnp.dot(pm, wp_ref[...], preferred_element_type=F32) * ps_ref[...]

    acc = jnp.dot(ya.astype(BF16), wo_ref[0:D_ATTN, :], preferred_element_type=F32)
    acc = acc + jnp.dot(yb.astype(BF16), wo_ref[D_ATTN:D_ATTN + D_POOL, :], preferred_element_type=F32)
    acc = acc + jnp.dot(yc_ref[...], wo_ref[D_ATTN + D_POOL:, :], preferred_element_type=F32)
    out_ref[...] = x_ref[...] + acc


def _mix_out(x2d, outs, lses, zb, yc, expand, w_pool_bd, pool_scale, w_out, tm, seq):
    n = x2d.shape[0]
    const = lambda i: (0, 0)
    row = lambda i: (i, 0)
    hb = tm // POOL_HALO
    last = n // POOL_HALO - 1
    return pl.pallas_call(
        functools.partial(_mix_out_kernel, tm=tm, seq=seq),
        grid=(n // tm,),
        in_specs=[
            pl.BlockSpec((tm, D_MODEL), row),
            pl.BlockSpec((tm, D_ATTN), row), pl.BlockSpec((tm, D_ATTN), row), pl.BlockSpec((tm, D_ATTN), row),
            pl.BlockSpec((tm, LANES), row), pl.BlockSpec((tm, LANES), row), pl.BlockSpec((tm, LANES), row),
            pl.BlockSpec((POOL_HALO, D_POOL), lambda i: (jnp.maximum(i * hb - 1, 0), 0)),
            pl.BlockSpec((tm, D_POOL), row),
            pl.BlockSpec((POOL_HALO, D_POOL), lambda i: (jnp.minimum((i + 1) * hb, last), 0)),
            pl.BlockSpec((tm, D_GATE), row),
            pl.BlockSpec((LANES, D_ATTN), const),
            pl.BlockSpec((D_POOL, D_POOL), const),
            pl.BlockSpec((1, D_POOL), const),
            pl.BlockSpec((D_MODEL, D_MODEL), const),
        ],
        out_specs=pl.BlockSpec((tm, D_MODEL), row),
        out_shape=jax.ShapeDtypeStruct((n, D_MODEL), F32),
        scratch_shapes=[pltpu.VMEM((tm + 2 * POOL_HALO, D_POOL), F32)],
        compiler_params=_cparams("parallel"),
        name="mix_out",
    )(x2d, outs[0], outs[1], outs[2], lses[0], lses[1], lses[2], zb, zb, zb, yc,
      expand, w_pool_bd, pool_scale, w_out)


def _mem_kv_kernel(m_ref, g_ref, w_ref, kv_ref):
    mn = _rms(m_ref[...], g_ref[...]).astype(BF16)
    kv_ref[...] = jnp.dot(mn, w_ref[...], preferred_element_type=F32).astype(BF16)


def _mem_kv(mem2d, norm_g, w_ckv):
    n = mem2d.shape[0]
    tm = N_MEM
    return pl.pallas_call(
        _mem_kv_kernel,
        grid=(n // tm,),
        in_specs=[
            pl.BlockSpec((tm, D_MODEL), lambda i: (i, 0)),
            pl.BlockSpec((1, D_MODEL), lambda i: (0, 0)),
            pl.BlockSpec((D_MODEL, 2 * D_MODEL), lambda i: (0, 0)),
        ],
        out_specs=pl.BlockSpec((tm, 2 * D_MODEL), lambda i: (i, 0)),
        out_shape=jax.ShapeDtypeStruct((n, 2 * D_MODEL), BF16),
        compiler_params=_cparams("parallel"),
        name="mem_kv",
    )(mem2d, norm_g, w_ckv)


def _route(hf, wr_hi_ref, wr_lo_ref, br_ref):
    tm = hf.shape[0]
    hi = hf.astype(BF16)
    lo = (hf - hi.astype(F32)).astype(BF16)
    lg = (jnp.dot(hi, wr_hi_ref[...], preferred_element_type=F32)
          + jnp.dot(lo, wr_hi_ref[...], preferred_element_type=F32)
          + jnp.dot(hi, wr_lo_ref[...], preferred_element_type=F32)) + br_ref[...]
    lane_f = lax.broadcasted_iota(jnp.int32, (tm, LANES), 1).astype(F32)
    big = jnp.float32(LANES)
    is_group = lane_f < N_GROUPS
    g_logit = jnp.where(is_group, lg, NEG_INF)
    g_max = jnp.max(g_logit, axis=-1, keepdims=True)
    g_idx = jnp.min(jnp.where(is_group & (g_logit == g_max), lane_f, big), axis=-1, keepdims=True)
    g_w = 1.0 / jnp.sum(jnp.where(is_group, jnp.exp(g_logit - g_max), 0.0), axis=-1, keepdims=True)
    first = N_GROUPS + EXPERTS_PER_GROUP * g_idx
    sel = (lane_f >= first) & (lane_f < first + EXPERTS_PER_GROUP)
    v1 = jnp.max(jnp.where(sel, lg, NEG_INF), axis=-1, keepdims=True)
    i1 = jnp.min(jnp.where(sel & (lg == v1), lane_f, big), axis=-1, keepdims=True)
    sel2 = sel & (lane_f != i1)
    v2 = jnp.max(jnp.where(sel2, lg, NEG_INF), axis=-1, keepdims=True)
    i2 = jnp.min(jnp.where(sel2 & (lg == v2), lane_f, big), axis=-1, keepdims=True)
    e2 = jnp.exp(v2 - v1)
    w1 = g_w / (1.0 + e2)
    w2 = g_w * e2 / (1.0 + e2)
    a = jnp.minimum(i1, i2) - first
    b = jnp.maximum(i1, i2) - first
    pair = a * (7.0 - a) * 0.5 + (b - a - 1.0)
    bucket = g_idx * PAIRS_PER_GROUP + pair
    first_is_low = i1 < i2
    return bucket, jnp.where(first_is_low, w1, w2), jnp.where(first_is_low, w2, w1)


def _cross_kernel(x_ref, g_ref, wq_ref, kv_ref, wo_ref, gf_ref, wrh_ref, wrl_ref, br_ref, tri_ref,
                  xaug_ref, cnt_ref, carry_ref, *, tm):
    x = x_ref[...]
    hc = _rms(x, g_ref[...]).astype(BF16)
    q = jnp.dot(hc, wq_ref[...], preferred_element_type=F32).astype(BF16)
    heads = []
    for h in range(N_CROSS_HEADS):
        cols = slice(h * CROSS_HEAD_DIM, (h + 1) * CROSS_HEAD_DIM)
        k = kv_ref[0, :, cols]
        v = kv_ref[0, :, D_MODEL + h * CROSS_HEAD_DIM:D_MODEL + (h + 1) * CROSS_HEAD_DIM]
        s = lax.dot_general(q[:, cols], k, (((1,), (1,)), ((), ())), preferred_element_type=F32)
        s = s * (CROSS_HEAD_DIM ** -0.5)
        m = jnp.max(s, axis=-1, keepdims=True)
        pe = jnp.exp(s - m)
        den = jnp.sum(pe, axis=-1, keepdims=True)
        heads.append((jnp.dot(pe.astype(BF16), v, preferred_element_type=F32) / den).astype(BF16))
    o = jnp.concatenate(heads, axis=-1)
    x2 = x + jnp.dot(o, wo_ref[...], preferred_element_type=F32)
    xaug_ref[:, :D_MODEL] = x2

    @pl.when(pl.program_id(0) == 0)
    def _():
        carry_ref[...] = jnp.zeros_like(carry_ref)

    bucket, w_lo, w_hi = _route(_rms(x2, gf_ref[...]), wrh_ref, wrl_ref, br_ref)
    lane_f = lax.broadcasted_iota(jnp.int32, (tm, LANES), 1).astype(F32)
    onehot = lane_f == bucket
    incl = jnp.dot(tri_ref[...], onehot.astype(BF16), preferred_element_type=F32)
    rank = jnp.sum(jnp.where(onehot, incl - 1.0 + carry_ref[...], 0.0), axis=-1, keepdims=True)
    carry_ref[...] += jnp.sum(onehot.astype(F32), axis=0, keepdims=True)
    cnt_ref[...] = carry_ref[...]
    slab = jnp.where(lane_f == ROUTE_BUCKET, bucket, 0.0)
    slab = jnp.where(lane_f == ROUTE_RANK, rank, slab)
    slab = jnp.where(lane_f == ROUTE_W_LO, w_lo, slab)
    slab = jnp.where(lane_f == ROUTE_W_HI, w_hi, slab)
    xaug_ref[:, D_MODEL:] = slab


def _cross(x2d, norm_g, w_cq, kv, w_co, norm_ffn, wr_hi, wr_lo, b_r, tri, tm, seq):
    n = x2d.shape[0]
    const = lambda i: (0, 0)
    per_seq = seq // tm
    return pl.pallas_call(
        functools.partial(_cross_kernel, tm=tm),
        grid=(n // tm,),
        in_specs=[
            pl.BlockSpec((tm, D_MODEL), lambda i: (i, 0)),
            pl.BlockSpec((1, D_MODEL), const),
            pl.BlockSpec((D_MODEL, D_MODEL), const),
            pl.BlockSpec((1, N_MEM, 2 * D_MODEL), lambda i: (i // per_seq, 0, 0)),
            pl.BlockSpec((D_MODEL, D_MODEL), const),
            pl.BlockSpec((1, D_MODEL), const),
            pl.BlockSpec((D_MODEL, LANES), const),
            pl.BlockSpec((D_MODEL, LANES), const),
            pl.BlockSpec((1, LANES), const),
            pl.BlockSpec((tm, tm), const),
        ],
        out_specs=[
            pl.BlockSpec((tm, D_AUG), lambda i: (i, 0)),
            pl.BlockSpec((1, LANES), const),
        ],
        out_shape=[
            jax.ShapeDtypeStruct((n, D_AUG), F32),
            jax.ShapeDtypeStruct((1, LANES), F32),
        ],
        scratch_shapes=[pltpu.VMEM((1, LANES), F32)],
        compiler_params=_cparams("arbitrary"),
        name="cross",
    )(x2d, norm_g, w_cq, kv, w_co, norm_ffn, wr_hi, wr_lo, b_r, tri)


def _moe_plan(xaug, counts, n_tiles):
    n = xaug.shape[0]
    bucket = xaug[:, D_MODEL + ROUTE_BUCKET].astype(jnp.int32)
    rank = xaug[:, D_MODEL + ROUTE_RANK].astype(jnp.int32)
    cnt = counts[0, :N_BUCKETS].astype(jnp.int32)
    tiles = (cnt + MOE_TILE - 1) // MOE_TILE
    tile_end = jnp.cumsum(tiles)
    tile_start = tile_end - tiles
    dest = tile_start[bucket] * MOE_TILE + rank
    src = jnp.zeros((n_tiles * MOE_TILE,), jnp.int32).at[dest].set(jnp.arange(n, dtype=jnp.int32))
    t = jnp.arange(n_tiles, dtype=jnp.int32)
    used = t < tile_end[-1]
    tb = jnp.minimum(jnp.searchsorted(tile_end, t, side="right").astype(jnp.int32), N_BUCKETS - 1)
    tb = jnp.where(used, tb, tb[jnp.maximum(tile_end[-1] - 1, 0)])
    n_valid = jnp.where(used, jnp.clip(cnt[tb] - (t - tile_start[tb]) * MOE_TILE, 0, MOE_TILE), 0)
    n_valid = jnp.concatenate([n_valid, jnp.zeros((1,), jnp.int32)])
    group, pair = tb // PAIRS_PER_GROUP, tb % PAIRS_PER_GROUP
    pair_lo = jnp.array([0, 0, 0, 1, 1, 2], jnp.int32)
    pair_hi = jnp.array([1, 2, 3, 2, 3, 3], jnp.int32)
    e_lo = group * EXPERTS_PER_GROUP + pair_lo[pair]
    e_hi = group * EXPERTS_PER_GROUP + pair_hi[pair]
    return e_lo, e_hi, n_valid, src.reshape(n_tiles, 1, MOE_TILE)


def _moe_kernel(elo_ref, ehi_ref, nv_ref, src_ref, nsrc_ref, xaug_hbm, g_ref, wu0_ref, wu1_ref, wd0_ref, wd1_ref,
                gf_ref, out_hbm, xin, yout, sem_in, sem_out, *, final):
    i = pl.program_id(0)
    nv = nv_ref[i]
    slot = i % 2

    def gather(rows_ref, s):
        def body(r, c):
            pltpu.make_async_copy(xaug_hbm.at[pl.ds(rows_ref[0, 0, r], 1)], xin.at[s, pl.ds(r, 1)],
                                  sem_in.at[s]).start()
            return c
        lax.fori_loop(0, MOE_TILE, body, 0)

    def wait_scatter(s, count):
        def body(r, c):
            pltpu.make_async_copy(yout.at[s, pl.ds(0, 1)], out_hbm.at[pl.ds(0, 1)], sem_out.at[s]).wait()
            return c
        lax.fori_loop(0, count, body, 0)

    @pl.when(nv > 0)
    def _():
        @pl.when(i == 0)
        def _():
            gather(src_ref, 0)

        @pl.when(nv_ref[i + 1] > 0)
        def _():
            gather(nsrc_ref, 1 - slot)

        pltpu.make_async_copy(xaug_hbm.at[pl.ds(0, MOE_TILE)], xin.at[slot], sem_in.at[slot]).wait()

        @pl.when(i >= 2)
        def _():
            wait_scatter(slot, nv_ref[i - 2])

        xa = xin[slot]
        xr = xa[:, :D_MODEL]
        slab = xa[:, D_MODEL:]
        lane = lax.broadcasted_iota(jnp.int32, (MOE_TILE, LANES), 1)
        w_lo = jnp.sum(jnp.where(lane == ROUTE_W_LO, slab, 0.0), axis=-1, keepdims=True)
        w_hi = jnp.sum(jnp.where(lane == ROUTE_W_HI, slab, 0.0), axis=-1, keepdims=True)
        hf = _rms(xr, g_ref[...]).astype(BF16)
        y = jnp.zeros((MOE_TILE, D_MODEL), F32)
        for wu_ref, wd_ref, w in ((wu0_ref, wd0_ref, w_lo), (wu1_ref, wd1_ref, w_hi)):
            gu = jnp.dot(hf, wu_ref[0], preferred_element_type=F32)
            gate_half, up_half = gu[:, :D_EXPERT], gu[:, D_EXPERT:]
            a = (gate_half * jax.nn.sigmoid(gate_half)) * up_half
            y = y + w * jnp.dot(a.astype(BF16), wd_ref[0], preferred_element_type=F32)
        res = xr + y
        if final:
            res = _rms(res, gf_ref[...])
        yout[slot] = res

        def scatter_row(r, c):
            pltpu.make_async_copy(yout.at[slot, pl.ds(r, 1)], out_hbm.at[pl.ds(src_ref[0, 0, r], 1)],
                                  sem_out.at[slot]).start()
            return c
        lax.fori_loop(0, nv, scatter_row, 0)

        @pl.when(nv_ref[i + 1] == 0)
        def _():
            @pl.when(i >= 1)
            def _():
                wait_scatter(1 - slot, nv_ref[i - 1])
            wait_scatter(slot, nv)


def _moe(xaug, counts, norm_g, w_up, w_down, norm_final, final):
    n = xaug.shape[0]
    n_tiles = n // MOE_TILE + N_BUCKETS
    e_lo, e_hi, n_valid, src = _moe_plan(xaug, counts, n_tiles)
    const = lambda i, lo, hi, nv: (0, 0)
    grid_spec = pltpu.PrefetchScalarGridSpec(
        num_scalar_prefetch=3,
        grid=(n_tiles,),
        in_specs=[
            pl.BlockSpec((1, 1, MOE_TILE), lambda i, lo, hi, nv: (i, 0, 0), memory_space=pltpu.SMEM),
            pl.BlockSpec((1, 1, MOE_TILE), lambda i, lo, hi, nv: (jnp.minimum(i + 1, n_tiles - 1), 0, 0),
                         memory_space=pltpu.SMEM),
            pl.BlockSpec(memory_space=pl.ANY),
            pl.BlockSpec((1, D_MODEL), const),
            pl.BlockSpec((1, D_MODEL, 2 * D_EXPERT), lambda i, lo, hi, nv: (lo[i], 0, 0)),
            pl.BlockSpec((1, D_MODEL, 2 * D_EXPERT), lambda i, lo, hi, nv: (hi[i], 0, 0)),
            pl.BlockSpec((1, D_EXPERT, D_MODEL), lambda i, lo, hi, nv: (lo[i], 0, 0)),
            pl.BlockSpec((1, D_EXPERT, D_MODEL), lambda i, lo, hi, nv: (hi[i], 0, 0)),
            pl.BlockSpec((1, D_MODEL), const),
        ],
        out_specs=pl.BlockSpec(memory_space=pl.ANY),
        scratch_shapes=[
            pltpu.VMEM((2, MOE_TILE, D_AUG), F32),
            pltpu.VMEM((2, MOE_TILE, D_MODEL), F32),
            pltpu.SemaphoreType.DMA((2,)),
            pltpu.SemaphoreType.DMA((2,)),
        ],
    )
    return pl.pallas_call(
        functools.partial(_moe_kernel, final=final),
        grid_spec=grid_spec,
        out_shape=jax.ShapeDtypeStruct((n, D_MODEL), F32),
        compiler_params=_cparams("arbitrary"),
        name="moe",
    )(e_lo, e_hi, n_valid, src, src, xaug, norm_g, w_up, w_up, w_down, w_down, norm_final)


def _prep_layer(p, l):
    b_full = jnp.repeat(jnp.transpose(p["b_spatial"][l]), D_GATE // N_GATE_GROUPS, axis=1)
    wp = p["w_pool"][l]
    gsz = D_POOL // len(POOL_WINDOWS)
    w_pool_bd = jnp.zeros((D_POOL, D_POOL), F32)
    for g in range(len(POOL_WINDOWS)):
        w_pool_bd = w_pool_bd.at[g * gsz:(g + 1) * gsz, g * gsz:(g + 1) * gsz].set(wp[g])
    w_r = jnp.zeros((D_MODEL, LANES), F32)
    w_r = w_r.at[:, :N_GROUPS].set(p["w_router_group"][l])
    w_r = w_r.at[:, N_GROUPS:N_GROUPS + N_EXPERTS].set(p["w_router_expert"][l])
    wr_hi = w_r.astype(BF16)
    wr_lo = (w_r - wr_hi.astype(F32)).astype(BF16)
    b_r = jnp.zeros((1, LANES), F32)
    b_r = b_r.at[0, :N_GROUPS].set(p["b_router_group"][l])
    b_r = b_r.at[0, N_GROUPS:N_GROUPS + N_EXPERTS].set(p["b_router_expert"][l])
    return dict(
        norm_mix=p["norm_mix"][l][None], w_in=p["w_in"][l].astype(BF16),
        gate_norm=p["gate_norm"][l][None], w_sp=p["w_spatial"][l].astype(BF16), b_full=b_full,
        w_pool_bd=w_pool_bd.astype(BF16), pool_scale=p["pool_scale"][l][None], w_out=p["w_out"][l].astype(BF16),
        norm_cross=p["norm_cross"][l][None], norm_mem=p["norm_mem"][l][None],
        w_cq=p["w_cq"][l].astype(BF16), w_ckv=p["w_ckv"][l].astype(BF16), w_co=p["w_co"][l].astype(BF16),
        norm_ffn=p["norm_ffn"][l][None], wr_hi=wr_hi, wr_lo=wr_lo, b_r=b_r,
        w_up=p["w_up"][l].astype(BF16), w_down=p["w_down"][l].astype(BF16),
    )


def _prepare(p):
    depth = p["norm_mix"].shape[0]
    head_of_lane = jnp.arange(D_ATTN) // HEAD_DIM
    return dict(
        layers=[_prep_layer(p, l) for l in range(depth)],
        biases=[_attn_bias(d) for d in DILATIONS],
        expand=(jnp.arange(LANES)[:, None] == head_of_lane[None, :]).astype(BF16),
        norm_final=p["norm_final"][None],
        tri=jnp.tril(jnp.ones((TOKEN_TILE, TOKEN_TILE), F32)).astype(BF16),
    )


def _trunk(x, mem, prep):
    batch, seq, _ = x.shape
    n = batch * seq
    x2d = x.reshape(n, D_MODEL)
    mem2d = mem.reshape(batch * N_MEM, D_MODEL)
    layers = prep["layers"]
    tm = TOKEN_TILE
    for li, lp in enumerate(layers):
        qkv, zb, yc = _mix_in(x2d, lp["norm_mix"], lp["w_in"], lp["gate_norm"], lp["w_sp"], lp["b_full"], tm)
        outs, lses = [], []
        for dil, bias in zip(DILATIONS, prep["biases"]):
            o, lse = _attn_pattern(qkv, bias, batch, seq, dil)
            outs.append(o)
            lses.append(lse)
        x2d = _mix_out(x2d, outs, lses, zb, yc, prep["expand"], lp["w_pool_bd"], lp["pool_scale"], lp["w_out"],
                       tm, seq)
        kv = _mem_kv(mem2d, lp["norm_mem"], lp["w_ckv"]).reshape(batch, N_MEM, 2 * D_MODEL)
        xaug, counts = _cross(x2d, lp["norm_cross"], lp["w_cq"], kv, lp["w_co"], lp["norm_ffn"], lp["wr_hi"],
                              lp["wr_lo"], lp["b_r"], prep["tri"], tm, seq)
        x2d = _moe(xaug, counts, lp["norm_ffn"], lp["w_up"], lp["w_down"], prep["norm_final"],
                   li == len(layers) - 1)
    return x2d.reshape(batch, seq, D_MODEL)


def kernel(x_prompt, x_sample, mem_prompt, mem_sample, norm_mix, w_in, w_pool, pool_scale, gate_norm, w_spatial, b_spatial, w_out, norm_cross, norm_mem, w_cq, w_ckv, w_co, norm_ffn, w_router_group, b_router_group, w_router_expert, b_router_expert, w_up, w_down, norm_final):
    prep = _prepare(dict(
        norm_mix=norm_mix, w_in=w_in, w_pool=w_pool, pool_scale=pool_scale, gate_norm=gate_norm,
        w_spatial=w_spatial, b_spatial=b_spatial, w_out=w_out, norm_cross=norm_cross, norm_mem=norm_mem,
        w_cq=w_cq, w_ckv=w_ckv, w_co=w_co, norm_ffn=norm_ffn, w_router_group=w_router_group,
        b_router_group=b_router_group, w_router_expert=w_router_expert, b_router_expert=b_router_expert,
        w_up=w_up, w_down=w_down, norm_final=norm_final))
    return (_trunk(x_prompt, mem_prompt, prep), _trunk(x_sample, mem_sample, prep))
```

```python
import functools

import jax
import jax.numpy as jnp
from jax import lax
from jax.experimental import pallas as pl
from jax.experimental.pallas import tpu as pltpu

F32 = jnp.float32
BF16 = jnp.bfloat16

D_MODEL = 1024
D_ATTN = 512
D_POOL = 256
D_GATE = 256
D_QKV = 3 * D_ATTN
D_IN = D_QKV + D_POOL + 2 * D_GATE
N_HEADS = 8
HEAD_DIM = 64
RADIUS = 64
DILATIONS = (1, 4, 16)
POOL_WINDOWS = (2, 4, 8, 16)
POOL_HALO = 8
CHUNK = 128
N_GATE_GROUPS = 4
N_MEM = 256
N_CROSS_HEADS = 4
CROSS_HEAD_DIM = 256
N_GROUPS = 4
EXPERTS_PER_GROUP = 4
N_EXPERTS = 16
D_EXPERT = 512
PAIRS_PER_GROUP = 6
N_BUCKETS = N_GROUPS * PAIRS_PER_GROUP
EPS = 1e-6
NEG_INF = -1e30

LANES = 128
Q_SUB = 128
K_WIN = Q_SUB + 2 * RADIUS
VMEM_LIMIT = 56 * 1024 * 1024
TOKEN_TILE = 512
MOE_TILE = 256
DMA_UNROLL = 8
D_AUG = D_MODEL + LANES
ROUTE_BUCKET, ROUTE_RANK, ROUTE_W_LO, ROUTE_W_HI = 0, 1, 2, 3


def _cparams(*sem):
    return pltpu.CompilerParams(dimension_semantics=sem, vmem_limit_bytes=VMEM_LIMIT)


def _rms(x, g):
    return (x * lax.rsqrt(jnp.mean(x * x, axis=-1, keepdims=True) + EPS)) * g


def _mix_in_kernel(x_ref, g_ref, win_ref, gn_ref, ws_ref, bs_ref, qkv_ref, zb_ref, yc_ref, *, tm):
    h = _rms(x_ref[...], g_ref[...]).astype(BF16)
    qkv_ref[...] = jnp.dot(h, win_ref[:, :D_QKV], preferred_element_type=F32).astype(BF16)
    zb_ref[...] = jnp.dot(h, win_ref[:, D_QKV:D_QKV + D_POOL], preferred_element_type=F32)
    u = jnp.dot(h, win_ref[:, D_QKV + D_POOL:D_QKV + D_POOL + D_GATE], preferred_element_type=F32)
    v = jnp.dot(h, win_ref[:, D_QKV + D_POOL + D_GATE:], preferred_element_type=F32)
    gu = jax.nn.gelu(u)
    vn = _rms(jax.nn.gelu(v), gn_ref[...]).astype(BF16)
    group = lax.broadcasted_iota(jnp.int32, (CHUNK, D_GATE), 1) // (D_GATE // N_GATE_GROUPS)
    for c in range(tm // CHUNK):
        vc = vn[c * CHUNK:(c + 1) * CHUNK]
        f = jnp.zeros((CHUNK, D_GATE), F32)
        for g in range(N_GATE_GROUPS):
            fg = jnp.dot(ws_ref[g], vc, preferred_element_type=F32)
            f = jnp.where(group == g, fg, f)
        yc_ref[c * CHUNK:(c + 1) * CHUNK, :] = (gu[c * CHUNK:(c + 1) * CHUNK] * (f + bs_ref[...])).astype(BF16)


def _mix_in(x2d, norm_g, w_in, gate_norm, w_sp, b_full, tm):
    n = x2d.shape[0]
    const = lambda i: (0, 0)
    return pl.pallas_call(
        functools.partial(_mix_in_kernel, tm=tm),
        grid=(n // tm,),
        in_specs=[
            pl.BlockSpec((tm, D_MODEL), lambda i: (i, 0)),
            pl.BlockSpec((1, D_MODEL), const),
            pl.BlockSpec((D_MODEL, D_IN), const),
            pl.BlockSpec((1, D_GATE), const),
            pl.BlockSpec((N_GATE_GROUPS, CHUNK, CHUNK), lambda i: (0, 0, 0)),
            pl.BlockSpec((CHUNK, D_GATE), const),
        ],
        out_specs=[
            pl.BlockSpec((tm, D_QKV), lambda i: (i, 0)),
            pl.BlockSpec((tm, D_POOL), lambda i: (i, 0)),
            pl.BlockSpec((tm, D_GATE), lambda i: (i, 0)),
        ],
        out_shape=[
            jax.ShapeDtypeStruct((n, D_QKV), BF16),
            jax.ShapeDtypeStruct((n, D_POOL), F32),
            jax.ShapeDtypeStruct((n, D_GATE), BF16),
        ],
        compiler_params=_cparams("parallel"),
        name="mix_in",
    )(x2d, norm_g, w_in, gate_norm, w_sp, b_full)


def _attn_kernel(q_ref, kp_ref, kc_ref, kn_ref, vp_ref, vc_ref, vn_ref, bias_ref, o_ref, lse_ref,
                 kw_ref, vw_ref, *, lq, seq):
    kw_ref[0:RADIUS] = kp_ref[0]
    kw_ref[RADIUS:RADIUS + lq] = kc_ref[0]
    kw_ref[RADIUS + lq:] = kn_ref[0]
    vw_ref[0:RADIUS] = vp_ref[0]
    vw_ref[RADIUS:RADIUS + lq] = vc_ref[0]
    vw_ref[RADIUS + lq:] = vn_ref[0]
    blk = pl.program_id(2)
    low_half = lax.broadcasted_iota(jnp.int32, (Q_SUB, LANES), 1) < HEAD_DIM
    lane = lax.broadcasted_iota(jnp.int32, (Q_SUB, LANES), 1)
    kcol = lax.broadcasted_iota(jnp.int32, (1, K_WIN), 1)

    def sub_block(j, carry):
        r0 = pl.multiple_of(j * Q_SUB, Q_SUB)
        kpos = kcol + (blk * lq + j * Q_SUB - RADIUS)
        cmask = jnp.where((kpos >= 0) & (kpos < seq), 0.0, NEG_INF).astype(F32)
        lse_tile = jnp.zeros((Q_SUB, LANES), F32)
        for p in range(N_HEADS // 2):
            cols = slice(p * LANES, (p + 1) * LANES)
            q = q_ref[0, pl.ds(r0, Q_SUB), cols]
            kw = kw_ref[pl.ds(r0, K_WIN), cols]
            vw = vw_ref[pl.ds(r0, K_WIN), cols]
            outs = []
            for hh in range(2):
                qm = jnp.where(low_half == (hh == 0), q, jnp.zeros_like(q))
                s = lax.dot_general(qm, kw, (((1,), (1,)), ((), ())), preferred_element_type=F32)
                s = s * (HEAD_DIM ** -0.5) + bias_ref[2 * p + hh] + cmask
                m = jnp.max(s, axis=-1, keepdims=True)
                pe = jnp.exp(s - m)
                den = jnp.sum(pe, axis=-1, keepdims=True)
                outs.append(jnp.dot(pe.astype(BF16), vw, preferred_element_type=F32) / den)
                lse_tile = jnp.where(lane == 2 * p + hh, m + jnp.log(den), lse_tile)
            o_ref[0, pl.ds(r0, Q_SUB), cols] = jnp.where(low_half, outs[0], outs[1]).astype(BF16)
        lse_ref[0, pl.ds(r0, Q_SUB), :] = lse_tile
        return carry

    lax.fori_loop(0, lq // Q_SUB, sub_block, 0)


def _attn_pattern(qkv, bias, batch, seq_full, dil):
    L = seq_full // dil
    lq = min(512, L)
    nblk = L // lq
    hb = lq // RADIUS
    last_halo = L // RADIUS - 1
    qkv_v = qkv.reshape(batch, L, dil * D_QKV)
    nq = D_QKV // D_ATTN
    prev = lambda i: jnp.maximum(i * hb - 1, 0)
    nxt = lambda i: jnp.minimum((i + 1) * hb, last_halo)
    o, lse = pl.pallas_call(
        functools.partial(_attn_kernel, lq=lq, seq=L),
        grid=(batch, dil, nblk),
        in_specs=[
            pl.BlockSpec((1, lq, D_ATTN), lambda b, c, i: (b, i, c * nq)),
            pl.BlockSpec((1, RADIUS, D_ATTN), lambda b, c, i: (b, prev(i), c * nq + 1)),
            pl.BlockSpec((1, lq, D_ATTN), lambda b, c, i: (b, i, c * nq + 1)),
            pl.BlockSpec((1, RADIUS, D_ATTN), lambda b, c, i: (b, nxt(i), c * nq + 1)),
            pl.BlockSpec((1, RADIUS, D_ATTN), lambda b, c, i: (b, prev(i), c * nq + 2)),
            pl.BlockSpec((1, lq, D_ATTN), lambda b, c, i: (b, i, c * nq + 2)),
            pl.BlockSpec((1, RADIUS, D_ATTN), lambda b, c, i: (b, nxt(i), c * nq + 2)),
            pl.BlockSpec((N_HEADS, Q_SUB, K_WIN), lambda b, c, i: (0, 0, 0)),
        ],
        out_specs=[
            pl.BlockSpec((1, lq, D_ATTN), lambda b, c, i: (b, i, c)),
            pl.BlockSpec((1, lq, LANES), lambda b, c, i: (b, i, c)),
        ],
        out_shape=[
            jax.ShapeDtypeStruct((batch, L, dil * D_ATTN), BF16),
            jax.ShapeDtypeStruct((batch, L, dil * LANES), F32),
        ],
        scratch_shapes=[
            pltpu.VMEM((lq + 2 * RADIUS, D_ATTN), BF16),
            pltpu.VMEM((lq + 2 * RADIUS, D_ATTN), BF16),
        ],
        compiler_params=_cparams("parallel", "parallel", "parallel"),
        name=f"attn_d{dil}",
    )(qkv_v, qkv_v, qkv_v, qkv_v, qkv_v, qkv_v, qkv_v, bias)
    n = batch * seq_full
    return o.reshape(n, D_ATTN), lse.reshape(n, LANES)


def _attn_bias(dil):
    slopes = jnp.exp2(-8.0 * jnp.arange(1, N_HEADS + 1, dtype=F32) / N_HEADS)
    rel = jnp.arange(K_WIN)[None, :] - RADIUS - jnp.arange(Q_SUB)[:, None]
    dist = (jnp.abs(rel) * dil).astype(F32)
    bias = -(slopes[:, None, None] * dist[None])
    return jnp.where((jnp.abs(rel) <= RADIUS)[None], bias, NEG_INF).astype(F32)


def _mix_out_kernel(x_ref, o1_ref, o2_ref, o3_ref, l1_ref, l2_ref, l3_ref, zp_ref, zc_ref, zn_ref, yc_ref,
                    ex_ref, wp_ref, ps_ref, wo_ref, out_ref, ext_ref, *, tm, seq):
    i = pl.program_id(0)
    l1, l2, l3 = l1_ref[...], l2_ref[...], l3_ref[...]
    mx = jnp.maximum(jnp.maximum(l1, l2), l3)
    e1, e2, e3 = jnp.exp(l1 - mx), jnp.exp(l2 - mx), jnp.exp(l3 - mx)
    den = e1 + e2 + e3
    ya = jnp.zeros((tm, D_ATTN), F32)
    for e, o_ref in ((e1, o1_ref), (e2, o2_ref), (e3, o3_ref)):
        a = jnp.dot((e / den).astype(BF16), ex_ref[...], preferred_element_type=F32)
        ya = ya + a * o_ref[...].astype(F32)

    pos0 = (i * tm) % seq
    zero_halo = jnp.zeros((POOL_HALO, D_POOL), F32)
    ext_ref[0:POOL_HALO] = jnp.where(pos0 == 0, zero_halo, zp_ref[...])
    ext_ref[POOL_HALO:POOL_HALO + tm] = zc_ref[...]
    ext_ref[POOL_HALO + tm:] = jnp.where(pos0 + tm == seq, zero_halo, zn_ref[...])
    pos = pos0 + lax.broadcasted_iota(jnp.int32, (tm, 1), 0)
    lane_group = lax.broadcasted_iota(jnp.int32, (tm, LANES), 1) // (D_POOL // len(POOL_WINDOWS))

    def shifted(off, slab):
        return ext_ref[POOL_HALO + off:POOL_HALO + off + tm, slab * LANES:(slab + 1) * LANES]

    def window_mean(total, w):
        cnt = jnp.minimum(pos + w // 2, seq) - jnp.maximum(pos - w // 2, 0)
        return total / cnt.astype(F32)

    pooled = []
    for slab in range(2):
        w_small, w_big = POOL_WINDOWS[2 * slab], POOL_WINDOWS[2 * slab + 1]
        s_small = shifted(-(w_small // 2), slab)
        for off in range(-(w_small // 2) + 1, w_small // 2):
            s_small = s_small + shifted(off, slab)
        s_big = s_small
        for off in list(range(-(w_big // 2), -(w_small // 2))) + list(range(w_small // 2, w_big // 2)):
            s_big = s_big + shifted(off, slab)
        mean = jnp.where(lane_group == 0, window_mean(s_small, w_small), window_mean(s_big, w_big))
        pooled.append(mean - shifted(0, slab))
    pm = jnp.concatenate(pooled, axis=-1).astype(BF16)
    yb = jnp.dot(pm, wp_ref[...], preferred_element_type=F32) * ps_ref[...]

    acc = jnp.dot(ya.astype(BF16), wo_ref[0:D_ATTN, :], preferred_element_type=F32)
    acc = acc + jnp.dot(yb.astype(BF16), wo_ref[D_ATTN:D_ATTN + D_POOL, :], preferred_element_type=F32)
    acc = acc + jnp.dot(yc_ref[...], wo_ref[D_ATTN + D_POOL:, :], preferred_element_type=F32)
    out_ref[...] = x_ref[...] + acc


def _mix_out(x2d, outs, lses, zb, yc, expand, w_pool_bd, pool_scale, w_out, tm, seq):
    n = x2d.shape[0]
    const = lambda i: (0, 0)
    row = lambda i: (i, 0)
    hb = tm // POOL_HALO
    last = n // POOL_HALO - 1
    return pl.pallas_call(
        functools.partial(_mix_out_kernel, tm=tm, seq=seq),
        grid=(n // tm,),
        in_specs=[
            pl.BlockSpec((tm, D_MODEL), row),
            pl.BlockSpec((tm, D_ATTN), row), pl.BlockSpec((tm, D_ATTN), row), pl.BlockSpec((tm, D_ATTN), row),
            pl.BlockSpec((tm, LANES), row), pl.BlockSpec((tm, LANES), row), pl.BlockSpec((tm, LANES), row),
            pl.BlockSpec((POOL_HALO, D_POOL), lambda i: (jnp.maximum(i * hb - 1, 0), 0)),
            pl.BlockSpec((tm, D_POOL), row),
            pl.BlockSpec((POOL_HALO, D_POOL), lambda i: (jnp.minimum((i + 1) * hb, last), 0)),
            pl.BlockSpec((tm, D_GATE), row),
            pl.BlockSpec((LANES, D_ATTN), const),
            pl.BlockSpec((D_POOL, D_POOL), const),
            pl.BlockSpec((1, D_POOL), const),
            pl.BlockSpec((D_MODEL, D_MODEL), const),
        ],
        out_specs=pl.BlockSpec((tm, D_MODEL), row),
        out_shape=jax.ShapeDtypeStruct((n, D_MODEL), F32),
        scratch_shapes=[pltpu.VMEM((tm + 2 * POOL_HALO, D_POOL), F32)],
        compiler_params=_cparams("parallel"),
        name="mix_out",
    )(x2d, outs[0], outs[1], outs[2], lses[0], lses[1], lses[2], zb, zb, zb, yc,
      expand, w_pool_bd, pool_scale, w_out)


def _mem_kv_kernel(m_ref, g_ref, w_ref, kv_ref):
    mn = _rms(m_ref[...], g_ref[...]).astype(BF16)
    kv_ref[...] = jnp.dot(mn, w_ref[...], preferred_element_type=F32).astype(BF16)


def _mem_kv(mem2d, norm_g, w_ckv):
    n = mem2d.shape[0]
    tm = N_MEM
    return pl.pallas_call(
        _mem_kv_kernel,
        grid=(n // tm,),
        in_specs=[
            pl.BlockSpec((tm, D_MODEL), lambda i: (i, 0)),
            pl.BlockSpec((1, D_MODEL), lambda i: (0, 0)),
            pl.BlockSpec((D_MODEL, 2 * D_MODEL), lambda i: (0, 0)),
        ],
        out_specs=pl.BlockSpec((tm, 2 * D_MODEL), lambda i: (i, 0)),
        out_shape=jax.ShapeDtypeStruct((n, 2 * D_MODEL), BF16),
        compiler_params=_cparams("parallel"),
        name="mem_kv",
    )(mem2d, norm_g, w_ckv)


def _route(hf, wr_hi_ref, wr_lo_ref, br_ref):
    tm = hf.shape[0]
    hi = hf.astype(BF16)
    lo = (hf - hi.astype(F32)).astype(BF16)
    lg = (jnp.dot(hi, wr_hi_ref[...], preferred_element_type=F32)
          + jnp.dot(lo, wr_hi_ref[...], preferred_element_type=F32)
          + jnp.dot(hi, wr_lo_ref[...], preferred_element_type=F32)) + br_ref[...]
    lane_f = lax.broadcasted_iota(jnp.int32, (tm, LANES), 1).astype(F32)
    big = jnp.float32(LANES)
    is_group = lane_f < N_GROUPS
    g_logit = jnp.where(is_group, lg, NEG_INF)
    g_max = jnp.max(g_logit, axis=-1, keepdims=True)
    g_idx = jnp.min(jnp.where(is_group & (g_logit == g_max), lane_f, big), axis=-1, keepdims=True)
    g_w = 1.0 / jnp.sum(jnp.where(is_group, jnp.exp(g_logit - g_max), 0.0), axis=-1, keepdims=True)
    first = N_GROUPS + EXPERTS_PER_GROUP * g_idx
    sel = (lane_f >= first) & (lane_f < first + EXPERTS_PER_GROUP)
    v1 = jnp.max(jnp.where(sel, lg, NEG_INF), axis=-1, keepdims=True)
    i1 = jnp.min(jnp.where(sel & (lg == v1), lane_f, big), axis=-1, keepdims=True)
    sel2 = sel & (lane_f != i1)
    v2 = jnp.max(jnp.where(sel2, lg, NEG_INF), axis=-1, keepdims=True)
    i2 = jnp.min(jnp.where(sel2 & (lg == v2), lane_f, big), axis=-1, keepdims=True)
    e2 = jnp.exp(v2 - v1)
    w1 = g_w / (1.0 + e2)
    w2 = g_w * e2 / (1.0 + e2)
    a = jnp.minimum(i1, i2) - first
    b = jnp.maximum(i1, i2) - first
    pair = a * (7.0 - a) * 0.5 + (b - a - 1.0)
    bucket = g_idx * PAIRS_PER_GROUP + pair
    first_is_low = i1 < i2
    return bucket, jnp.where(first_is_low, w1, w2), jnp.where(first_is_low, w2, w1)


def _cross_kernel(x_ref, g_ref, wq_ref, kv_ref, wo_ref, gf_ref, wrh_ref, wrl_ref, br_ref, tri_ref,
                  xaug_ref, cnt_ref, carry_ref, *, tm):
    x = x_ref[...]
    hc = _rms(x, g_ref[...]).astype(BF16)
    q = jnp.dot(hc, wq_ref[...], preferred_element_type=F32).astype(BF16)
    heads = []
    for h in range(N_CROSS_HEADS):
        cols = slice(h * CROSS_HEAD_DIM, (h + 1) * CROSS_HEAD_DIM)
        k = kv_ref[0, :, cols]
        v = kv_ref[0, :, D_MODEL + h * CROSS_HEAD_DIM:D_MODEL + (h + 1) * CROSS_HEAD_DIM]
        s = lax.dot_general(q[:, cols], k, (((1,), (1,)), ((), ())), preferred_element_type=F32)
        s = s * (CROSS_HEAD_DIM ** -0.5)
        m = jnp.max(s, axis=-1, keepdims=True)
        pe = jnp.exp(s - m)
        den = jnp.sum(pe, axis=-1, keepdims=True)
        heads.append((jnp.dot(pe.astype(BF16), v, preferred_element_type=F32) / den).astype(BF16))
    o = jnp.concatenate(heads, axis=-1)
    x2 = x + jnp.dot(o, wo_ref[...], preferred_element_type=F32)
    xaug_ref[:, :D_MODEL] = x2

    @pl.when(pl.program_id(0) == 0)
    def _():
        carry_ref[...] = jnp.zeros_like(carry_ref)

    bucket, w_lo, w_hi = _route(_rms(x2, gf_ref[...]), wrh_ref, wrl_ref, br_ref)
    lane_f = lax.broadcasted_iota(jnp.int32, (tm, LANES), 1).astype(F32)
    onehot = lane_f == bucket
    incl = jnp.dot(tri_ref[...], onehot.astype(BF16), preferred_element_type=F32)
    rank = jnp.sum(jnp.where(onehot, incl - 1.0 + carry_ref[...], 0.0), axis=-1, keepdims=True)
    carry_ref[...] += jnp.sum(onehot.astype(F32), axis=0, keepdims=True)
    cnt_ref[...] = carry_ref[...]
    slab = jnp.where(lane_f == ROUTE_BUCKET, bucket, 0.0)
    slab = jnp.where(lane_f == ROUTE_RANK, rank, slab)
    slab = jnp.where(lane_f == ROUTE_W_LO, w_lo, slab)
    slab = jnp.where(lane_f == ROUTE_W_HI, w_hi, slab)
    xaug_ref[:, D_MODEL:] = slab


def _cross(x2d, norm_g, w_cq, kv, w_co, norm_ffn, wr_hi, wr_lo, b_r, tri, tm, seq):
    n = x2d.shape[0]
    const = lambda i: (0, 0)
    per_seq = seq // tm
    return pl.pallas_call(
        functools.partial(_cross_kernel, tm=tm),
        grid=(n // tm,),
        in_specs=[
            pl.BlockSpec((tm, D_MODEL), lambda i: (i, 0)),
            pl.BlockSpec((1, D_MODEL), const),
            pl.BlockSpec((D_MODEL, D_MODEL), const),
            pl.BlockSpec((1, N_MEM, 2 * D_MODEL), lambda i: (i // per_seq, 0, 0)),
            pl.BlockSpec((D_MODEL, D_MODEL), const),
            pl.BlockSpec((1, D_MODEL), const),
            pl.BlockSpec((D_MODEL, LANES), const),
            pl.BlockSpec((D_MODEL, LANES), const),
            pl.BlockSpec((1, LANES), const),
            pl.BlockSpec((tm, tm), const),
        ],
        out_specs=[
            pl.BlockSpec((tm, D_AUG), lambda i: (i, 0)),
            pl.BlockSpec((1, LANES), const),
        ],
        out_shape=[
            jax.ShapeDtypeStruct((n, D_AUG), F32),
            jax.ShapeDtypeStruct((1, LANES), F32),
        ],
        scratch_shapes=[pltpu.VMEM((1, LANES), F32)],
        compiler_params=_cparams("arbitrary"),
        name="cross",
    )(x2d, norm_g, w_cq, kv, w_co, norm_ffn, wr_hi, wr_lo, b_r, tri)


def _moe_plan(xaug, counts, n_tiles):
    n = xaug.shape[0]
    bucket = xaug[:, D_MODEL + ROUTE_BUCKET].astype(jnp.int32)
    rank = xaug[:, D_MODEL + ROUTE_RANK].astype(jnp.int32)
    cnt = counts[0, :N_BUCKETS].astype(jnp.int32)
    tiles = (cnt + MOE_TILE - 1) // MOE_TILE
    tile_end = jnp.cumsum(tiles)
    tile_start = tile_end - tiles
    dest = tile_start[bucket] * MOE_TILE + rank
    src = jnp.zeros((n_tiles * MOE_TILE,), jnp.int32).at[dest].set(jnp.arange(n, dtype=jnp.int32))
    t = jnp.arange(n_tiles, dtype=jnp.int32)
    used = t < tile_end[-1]
    tb = jnp.minimum(jnp.searchsorted(tile_end, t, side="right").astype(jnp.int32), N_BUCKETS - 1)
    tb = jnp.where(used, tb, tb[jnp.maximum(tile_end[-1] - 1, 0)])
    n_valid = jnp.where(used, jnp.clip(cnt[tb] - (t - tile_start[tb]) * MOE_TILE, 0, MOE_TILE), 0)
    n_valid = jnp.concatenate([n_valid, jnp.zeros((1,), jnp.int32)])
    group, pair = tb // PAIRS_PER_GROUP, tb % PAIRS_PER_GROUP
    pair_lo = jnp.array([0, 0, 0, 1, 1, 2], jnp.int32)
    pair_hi = jnp.array([1, 2, 3, 2, 3, 3], jnp.int32)
    e_lo = group * EXPERTS_PER_GROUP + pair_lo[pair]
    e_hi = group * EXPERTS_PER_GROUP + pair_hi[pair]
    return e_lo, e_hi, n_valid, src.reshape(n_tiles, 1, MOE_TILE)


def _moe_kernel(elo_ref, ehi_ref, nv_ref, src_ref, nsrc_ref, xaug_hbm, g_ref, wu0_ref, wu1_ref, wd0_ref, wd1_ref,
                gf_ref, out_hbm, xin, yout, sem_in, sem_out, *, final):
    i = pl.program_id(0)
    nv = nv_ref[i]
    slot = i % 2

    def gather(rows_ref, s):
        def body(r, c):
            pltpu.make_async_copy(xaug_hbm.at[pl.ds(rows_ref[0, 0, r], 1)], xin.at[s, pl.ds(r, 1)],
                                  sem_in.at[s]).start()
            return c
        lax.fori_loop(0, MOE_TILE, body, 0, unroll=DMA_UNROLL)

    def wait_scatter(s, count):
        @pl.when(count == MOE_TILE)
        def _():
            pltpu.make_async_copy(yout.at[s], out_hbm.at[pl.ds(0, MOE_TILE)], sem_out.at[s]).wait()

        @pl.when(count != MOE_TILE)
        def _():
            def body(r, c):
                pltpu.make_async_copy(yout.at[s, pl.ds(0, 1)], out_hbm.at[pl.ds(0, 1)], sem_out.at[s]).wait()
                return c
            lax.fori_loop(0, count, body, 0)

    @pl.when(nv > 0)
    def _():
        @pl.when(i == 0)
        def _():
            gather(src_ref, 0)

        @pl.when(nv_ref[i + 1] > 0)
        def _():
            gather(nsrc_ref, 1 - slot)

        pltpu.make_async_copy(xaug_hbm.at[pl.ds(0, MOE_TILE)], xin.at[slot], sem_in.at[slot]).wait()

        @pl.when(i >= 2)
        def _():
            wait_scatter(slot, nv_ref[i - 2])

        xa = xin[slot]
        xr = xa[:, :D_MODEL]
        slab = xa[:, D_MODEL:]
        lane = lax.broadcasted_iota(jnp.int32, (MOE_TILE, LANES), 1)
        w_lo = jnp.sum(jnp.where(lane == ROUTE_W_LO, slab, 0.0), axis=-1, keepdims=True)
        w_hi = jnp.sum(jnp.where(lane == ROUTE_W_HI, slab, 0.0), axis=-1, keepdims=True)
        hf = _rms(xr, g_ref[...]).astype(BF16)
        y = jnp.zeros((MOE_TILE, D_MODEL), F32)
        for wu_ref, wd_ref, w in ((wu0_ref, wd0_ref, w_lo), (wu1_ref, wd1_ref, w_hi)):
            gu = jnp.dot(hf, wu_ref[0], preferred_element_type=F32)
            gate_half, up_half = gu[:, :D_EXPERT], gu[:, D_EXPERT:]
            a = (gate_half * jax.nn.sigmoid(gate_half)) * up_half
            y = y + w * jnp.dot(a.astype(BF16), wd_ref[0], preferred_element_type=F32)
        res = xr + y
        if final:
            res = _rms(res, gf_ref[...])
        yout[slot] = res

        def scatter_row(r, c):
            pltpu.make_async_copy(yout.at[slot, pl.ds(r, 1)], out_hbm.at[pl.ds(src_ref[0, 0, r], 1)],
                                  sem_out.at[slot]).start()
            return c

        @pl.when(nv == MOE_TILE)
        def _():
            lax.fori_loop(0, MOE_TILE, scatter_row, 0, unroll=DMA_UNROLL)

        @pl.when(nv != MOE_TILE)
        def _():
            lax.fori_loop(0, nv, scatter_row, 0)

        @pl.when(nv_ref[i + 1] == 0)
        def _():
            @pl.when(i >= 1)
            def _():
                wait_scatter(1 - slot, nv_ref[i - 1])
            wait_scatter(slot, nv)


def _moe(xaug, counts, norm_g, w_up, w_down, norm_final, final):
    n = xaug.shape[0]
    n_tiles = n // MOE_TILE + N_BUCKETS
    e_lo, e_hi, n_valid, src = _moe_plan(xaug, counts, n_tiles)
    const = lambda i, lo, hi, nv: (0, 0)
    grid_spec = pltpu.PrefetchScalarGridSpec(
        num_scalar_prefetch=3,
        grid=(n_tiles,),
        in_specs=[
            pl.BlockSpec((1, 1, MOE_TILE), lambda i, lo, hi, nv: (i, 0, 0), memory_space=pltpu.SMEM),
            pl.BlockSpec((1, 1, MOE_TILE), lambda i, lo, hi, nv: (jnp.minimum(i + 1, n_tiles - 1), 0, 0),
                         memory_space=pltpu.SMEM),
            pl.BlockSpec(memory_space=pl.ANY),
            pl.BlockSpec((1, D_MODEL), const),
            pl.BlockSpec((1, D_MODEL, 2 * D_EXPERT), lambda i, lo, hi, nv: (lo[i], 0, 0)),
            pl.BlockSpec((1, D_MODEL, 2 * D_EXPERT), lambda i, lo, hi, nv: (hi[i], 0, 0)),
            pl.BlockSpec((1, D_EXPERT, D_MODEL), lambda i, lo, hi, nv: (lo[i], 0, 0)),
            pl.BlockSpec((1, D_EXPERT, D_MODEL), lambda i, lo, hi, nv: (hi[i], 0, 0)),
            pl.BlockSpec((1, D_MODEL), const),
        ],
        out_specs=pl.BlockSpec(memory_space=pl.ANY),
        scratch_shapes=[
            pltpu.VMEM((2, MOE_TILE, D_AUG), F32),
            pltpu.VMEM((2, MOE_TILE, D_MODEL), F32),
            pltpu.SemaphoreType.DMA((2,)),
            pltpu.SemaphoreType.DMA((2,)),
        ],
    )
    return pl.pallas_call(
        functools.partial(_moe_kernel, final=final),
        grid_spec=grid_spec,
        out_shape=jax.ShapeDtypeStruct((n, D_MODEL), F32),
        compiler_params=_cparams("arbitrary"),
        name="moe",
    )(e_lo, e_hi, n_valid, src, src, xaug, norm_g, w_up, w_up, w_down, w_down, norm_final)


def _prep_layer(p, l):
    b_full = jnp.repeat(jnp.transpose(p["b_spatial"][l]), D_GATE // N_GATE_GROUPS, axis=1)
    wp = p["w_pool"][l]
    gsz = D_POOL // len(POOL_WINDOWS)
    w_pool_bd = jnp.zeros((D_POOL, D_POOL), F32)
    for g in range(len(POOL_WINDOWS)):
        w_pool_bd = w_pool_bd.at[g * gsz:(g + 1) * gsz, g * gsz:(g + 1) * gsz].set(wp[g])
    w_r = jnp.zeros((D_MODEL, LANES), F32)
    w_r = w_r.at[:, :N_GROUPS].set(p["w_router_group"][l])
    w_r = w_r.at[:, N_GROUPS:N_GROUPS + N_EXPERTS].set(p["w_router_expert"][l])
    wr_hi = w_r.astype(BF16)
    wr_lo = (w_r - wr_hi.astype(F32)).astype(BF16)
    b_r = jnp.zeros((1, LANES), F32)
    b_r = b_r.at[0, :N_GROUPS].set(p["b_router_group"][l])
    b_r = b_r.at[0, N_GROUPS:N_GROUPS + N_EXPERTS].set(p["b_router_expert"][l])
    return dict(
        norm_mix=p["norm_mix"][l][None], w_in=p["w_in"][l].astype(BF16),
        gate_norm=p["gate_norm"][l][None], w_sp=p["w_spatial"][l].astype(BF16), b_full=b_full,
        w_pool_bd=w_pool_bd.astype(BF16), pool_scale=p["pool_scale"][l][None], w_out=p["w_out"][l].astype(BF16),
        norm_cross=p["norm_cross"][l][None], norm_mem=p["norm_mem"][l][None],
        w_cq=p["w_cq"][l].astype(BF16), w_ckv=p["w_ckv"][l].astype(BF16), w_co=p["w_co"][l].astype(BF16),
        norm_ffn=p["norm_ffn"][l][None], wr_hi=wr_hi, wr_lo=wr_lo, b_r=b_r,
        w_up=p["w_up"][l].astype(BF16), w_down=p["w_down"][l].astype(BF16),
    )


def _prepare(p):
    depth = p["norm_mix"].shape[0]
    head_of_lane = jnp.arange(D_ATTN) // HEAD_DIM
    return dict(
        layers=[_prep_layer(p, l) for l in range(depth)],
        biases=[_attn_bias(d) for d in DILATIONS],
        expand=(jnp.arange(LANES)[:, None] == head_of_lane[None, :]).astype(BF16),
        norm_final=p["norm_final"][None],
        tri=jnp.tril(jnp.ones((TOKEN_TILE, TOKEN_TILE), F32)).astype(BF16),
    )


def _trunk(x, mem, prep):
    batch, seq, _ = x.shape
    n = batch * seq
    x2d = x.reshape(n, D_MODEL)
    mem2d = mem.reshape(batch * N_MEM, D_MODEL)
    layers = prep["layers"]
    tm = TOKEN_TILE
    for li, lp in enumerate(layers):
        qkv, zb, yc = _mix_in(x2d, lp["norm_mix"], lp["w_in"], lp["gate_norm"], lp["w_sp"], lp["b_full"], tm)
        outs, lses = [], []
        for dil, bias in zip(DILATIONS, prep["biases"]):
            o, lse = _attn_pattern(qkv, bias, batch, seq, dil)
            outs.append(o)
            lses.append(lse)
        x2d = _mix_out(x2d, outs, lses, zb, yc, prep["expand"], lp["w_pool_bd"], lp["pool_scale"], lp["w_out"],
                       tm, seq)
        kv = _mem_kv(mem2d, lp["norm_mem"], lp["w_ckv"]).reshape(batch, N_MEM, 2 * D_MODEL)
        xaug, counts = _cross(x2d, lp["norm_cross"], lp["w_cq"], kv, lp["w_co"], lp["norm_ffn"], lp["wr_hi"],
                              lp["wr_lo"], lp["b_r"], prep["tri"], tm, seq)
        x2d = _moe(xaug, counts, lp["norm_ffn"], lp["w_up"], lp["w_down"], prep["norm_final"],
                   li == len(layers) - 1)
    return x2d.reshape(batch, seq, D_MODEL)


def kernel(x_prompt, x_sample, mem_prompt, mem_sample, norm_mix, w_in, w_pool, pool_scale, gate_norm, w_spatial, b_spatial, w_out, norm_cross, norm_mem, w_cq, w_ckv, w_co, norm_ffn, w_router_group, b_router_group, w_router_expert, b_router_expert, w_up, w_down, norm_final):
    prep = _prepare(dict(
        norm_mix=norm_mix, w_in=w_in, w_pool=w_pool, pool_scale=pool_scale, gate_norm=gate_norm,
        w_spatial=w_spatial, b_spatial=b_spatial, w_out=w_out, norm_cross=norm_cross, norm_mem=norm_mem,
        w_cq=w_cq, w_ckv=w_ckv, w_co=w_co, norm_ffn=norm_ffn, w_router_group=w_router_group,
        b_router_group=b_router_group, w_router_expert=w_router_expert, b_router_expert=b_router_expert,
        w_up=w_up, w_down=w_down, norm_final=norm_final))
    return (_trunk(x_prompt, mem_prompt, prep), _trunk(x_sample, mem_sample, prep))
```

```python
import functools

import jax
import jax.numpy as jnp
from jax import lax
from jax.experimental import pallas as pl
from jax.experimental.pallas import tpu as pltpu

F32 = jnp.float32
BF16 = jnp.bfloat16

D_MODEL = 1024
D_ATTN = 512
D_POOL = 256
D_GATE = 256
D_QKV = 3 * D_ATTN
D_IN = D_QKV + D_POOL + 2 * D_GATE
N_HEADS = 8
HEAD_DIM = 64
RADIUS = 64
DILATIONS = (1, 4, 16)
POOL_WINDOWS = (2, 4, 8, 16)
POOL_HALO = 8
CHUNK = 128
N_GATE_GROUPS = 4
N_MEM = 256
N_CROSS_HEADS = 4
CROSS_HEAD_DIM = 256
N_GROUPS = 4
EXPERTS_PER_GROUP = 4
N_EXPERTS = 16
D_EXPERT = 512
PAIRS_PER_GROUP = 6
N_BUCKETS = N_GROUPS * PAIRS_PER_GROUP
EPS = 1e-6
NEG_INF = -1e30

LANES = 128
Q_SUB = 128
K_WIN = Q_SUB + 2 * RADIUS
VMEM_LIMIT = 56 * 1024 * 1024
TOKEN_TILE = 512
MOE_TILE = 256
DMA_UNROLL = 8
D_AUG = D_MODEL + LANES
ROUTE_BUCKET, ROUTE_RANK, ROUTE_W_LO, ROUTE_W_HI = 0, 1, 2, 3


def _cparams(*sem):
    return pltpu.CompilerParams(dimension_semantics=sem, vmem_limit_bytes=VMEM_LIMIT)


def _rms(x, g):
    return (x * lax.rsqrt(jnp.mean(x * x, axis=-1, keepdims=True) + EPS)) * g


def _mix_in_kernel(x_ref, g_ref, win_ref, gn_ref, ws_ref, bs_ref, qkv_ref, zb_ref, yc_ref, *, tm):
    h = _rms(x_ref[...], g_ref[...]).astype(BF16)
    qkv_ref[...] = jnp.dot(h, win_ref[:, :D_QKV], preferred_element_type=F32).astype(BF16)
    zb_ref[...] = jnp.dot(h, win_ref[:, D_QKV:D_QKV + D_POOL], preferred_element_type=F32)
    u = jnp.dot(h, win_ref[:, D_QKV + D_POOL:D_QKV + D_POOL + D_GATE], preferred_element_type=F32)
    v = jnp.dot(h, win_ref[:, D_QKV + D_POOL + D_GATE:], preferred_element_type=F32)
    gu = jax.nn.gelu(u)
    vn = _rms(jax.nn.gelu(v), gn_ref[...]).astype(BF16)
    group = lax.broadcasted_iota(jnp.int32, (CHUNK, D_GATE), 1) // (D_GATE // N_GATE_GROUPS)
    for c in range(tm // CHUNK):
        vc = vn[c * CHUNK:(c + 1) * CHUNK]
        f = jnp.zeros((CHUNK, D_GATE), F32)
        for g in range(N_GATE_GROUPS):
            fg = jnp.dot(ws_ref[g], vc, preferred_element_type=F32)
            f = jnp.where(group == g, fg, f)
        yc_ref[c * CHUNK:(c + 1) * CHUNK, :] = (gu[c * CHUNK:(c + 1) * CHUNK] * (f + bs_ref[...])).astype(BF16)


def _mix_in(x2d, norm_g, w_in, gate_norm, w_sp, b_full, tm):
    n = x2d.shape[0]
    const = lambda i: (0, 0)
    return pl.pallas_call(
        functools.partial(_mix_in_kernel, tm=tm),
        grid=(n // tm,),
        in_specs=[
            pl.BlockSpec((tm, D_MODEL), lambda i: (i, 0)),
            pl.BlockSpec((1, D_MODEL), const),
            pl.BlockSpec((D_MODEL, D_IN), const),
            pl.BlockSpec((1, D_GATE), const),
            pl.BlockSpec((N_GATE_GROUPS, CHUNK, CHUNK), lambda i: (0, 0, 0)),
            pl.BlockSpec((CHUNK, D_GATE), const),
        ],
        out_specs=[
            pl.BlockSpec((tm, D_QKV), lambda i: (i, 0)),
            pl.BlockSpec((tm, D_POOL), lambda i: (i, 0)),
            pl.BlockSpec((tm, D_GATE), lambda i: (i, 0)),
        ],
        out_shape=[
            jax.ShapeDtypeStruct((n, D_QKV), BF16),
            jax.ShapeDtypeStruct((n, D_POOL), F32),
            jax.ShapeDtypeStruct((n, D_GATE), BF16),
        ],
        compiler_params=_cparams("parallel"),
        name="mix_in",
    )(x2d, norm_g, w_in, gate_norm, w_sp, b_full)


def _attn_kernel(q_ref, kp_ref, kc_ref, kn_ref, vp_ref, vc_ref, vn_ref, bias_ref, o_ref, lse_ref,
                 kw_ref, vw_ref, *, lq, seq):
    kw_ref[0:RADIUS] = kp_ref[0]
    kw_ref[RADIUS:RADIUS + lq] = kc_ref[0]
    kw_ref[RADIUS + lq:] = kn_ref[0]
    vw_ref[0:RADIUS] = vp_ref[0]
    vw_ref[RADIUS:RADIUS + lq] = vc_ref[0]
    vw_ref[RADIUS + lq:] = vn_ref[0]
    blk = pl.program_id(2)
    low_half = lax.broadcasted_iota(jnp.int32, (Q_SUB, LANES), 1) < HEAD_DIM
    lane = lax.broadcasted_iota(jnp.int32, (Q_SUB, LANES), 1)
    kcol = lax.broadcasted_iota(jnp.int32, (1, K_WIN), 1)

    def sub_block(j, carry):
        r0 = pl.multiple_of(j * Q_SUB, Q_SUB)
        kpos = kcol + (blk * lq + j * Q_SUB - RADIUS)
        cmask = jnp.where((kpos >= 0) & (kpos < seq), 0.0, NEG_INF).astype(F32)
        lse_tile = jnp.zeros((Q_SUB, LANES), F32)
        for p in range(N_HEADS // 2):
            cols = slice(p * LANES, (p + 1) * LANES)
            q = q_ref[0, pl.ds(r0, Q_SUB), cols]
            kw = kw_ref[pl.ds(r0, K_WIN), cols]
            vw = vw_ref[pl.ds(r0, K_WIN), cols]
            outs = []
            for hh in range(2):
                qm = jnp.where(low_half == (hh == 0), q, jnp.zeros_like(q))
                s = lax.dot_general(qm, kw, (((1,), (1,)), ((), ())), preferred_element_type=F32)
                s = s * (HEAD_DIM ** -0.5) + bias_ref[2 * p + hh] + cmask
                m = jnp.max(s, axis=-1, keepdims=True)
                pe = jnp.exp(s - m)
                den = jnp.sum(pe, axis=-1, keepdims=True)
                outs.append(jnp.dot(pe.astype(BF16), vw, preferred_element_type=F32) / den)
                lse_tile = jnp.where(lane == 2 * p + hh, m + jnp.log(den), lse_tile)
            o_ref[0, pl.ds(r0, Q_SUB), cols] = jnp.where(low_half, outs[0], outs[1]).astype(BF16)
        lse_ref[0, pl.ds(r0, Q_SUB), :] = lse_tile
        return carry

    lax.fori_loop(0, lq // Q_SUB, sub_block, 0)


def _attn_pattern(qkv, bias, batch, seq_full, dil):
    L = seq_full // dil
    lq = min(512, L)
    nblk = L // lq
    hb = lq // RADIUS
    last_halo = L // RADIUS - 1
    qkv_v = qkv.reshape(batch, L, dil * D_QKV)
    nq = D_QKV // D_ATTN
    prev = lambda i: jnp.maximum(i * hb - 1, 0)
    nxt = lambda i: jnp.minimum((i + 1) * hb, last_halo)
    o, lse = pl.pallas_call(
        functools.partial(_attn_kernel, lq=lq, seq=L),
        grid=(batch, dil, nblk),
        in_specs=[
            pl.BlockSpec((1, lq, D_ATTN), lambda b, c, i: (b, i, c * nq)),
            pl.BlockSpec((1, RADIUS, D_ATTN), lambda b, c, i: (b, prev(i), c * nq + 1)),
            pl.BlockSpec((1, lq, D_ATTN), lambda b, c, i: (b, i, c * nq + 1)),
            pl.BlockSpec((1, RADIUS, D_ATTN), lambda b, c, i: (b, nxt(i), c * nq + 1)),
            pl.BlockSpec((1, RADIUS, D_ATTN), lambda b, c, i: (b, prev(i), c * nq + 2)),
            pl.BlockSpec((1, lq, D_ATTN), lambda b, c, i: (b, i, c * nq + 2)),
            pl.BlockSpec((1, RADIUS, D_ATTN), lambda b, c, i: (b, nxt(i), c * nq + 2)),
            pl.BlockSpec((N_HEADS, Q_SUB, K_WIN), lambda b, c, i: (0, 0, 0)),
        ],
        out_specs=[
            pl.BlockSpec((1, lq, D_ATTN), lambda b, c, i: (b, i, c)),
            pl.BlockSpec((1, lq, LANES), lambda b, c, i: (b, i, c)),
        ],
        out_shape=[
            jax.ShapeDtypeStruct((batch, L, dil * D_ATTN), BF16),
            jax.ShapeDtypeStruct((batch, L, dil * LANES), F32),
        ],
        scratch_shapes=[
            pltpu.VMEM((lq + 2 * RADIUS, D_ATTN), BF16),
            pltpu.VMEM((lq + 2 * RADIUS, D_ATTN), BF16),
        ],
        compiler_params=_cparams("parallel", "parallel", "parallel"),
        name=f"attn_d{dil}",
    )(qkv_v, qkv_v, qkv_v, qkv_v, qkv_v, qkv_v, qkv_v, bias)
    n = batch * seq_full
    return o.reshape(n, D_ATTN), lse.reshape(n, LANES)


def _attn_bias(dil):
    slopes = jnp.exp2(-8.0 * jnp.arange(1, N_HEADS + 1, dtype=F32) / N_HEADS)
    rel = jnp.arange(K_WIN)[None, :] - RADIUS - jnp.arange(Q_SUB)[:, None]
    dist = (jnp.abs(rel) * dil).astype(F32)
    bias = -(slopes[:, None, None] * dist[None])
    return jnp.where((jnp.abs(rel) <= RADIUS)[None], bias, NEG_INF).astype(F32)


def _mix_out_kernel(x_ref, o1_ref, o2_ref, o3_ref, l1_ref, l2_ref, l3_ref, zp_ref, zc_ref, zn_ref, yc_ref,
                    ex_ref, wp_ref, ps_ref, wo_ref, out_ref, ext_ref, *, tm, seq):
    i = pl.program_id(0)
    l1, l2, l3 = l1_ref[...], l2_ref[...], l3_ref[...]
    mx = jnp.maximum(jnp.maximum(l1, l2), l3)
    e1, e2, e3 = jnp.exp(l1 - mx), jnp.exp(l2 - mx), jnp.exp(l3 - mx)
    den = e1 + e2 + e3
    ya = jnp.zeros((tm, D_ATTN), F32)
    for e, o_ref in ((e1, o1_ref), (e2, o2_ref), (e3, o3_ref)):
        a = jnp.dot((e / den).astype(BF16), ex_ref[...], preferred_element_type=F32)
        ya = ya + a * o_ref[...].astype(F32)

    pos0 = (i * tm) % seq
    zero_halo = jnp.zeros((POOL_HALO, D_POOL), F32)
    ext_ref[0:POOL_HALO] = jnp.where(pos0 == 0, zero_halo, zp_ref[...])
    ext_ref[POOL_HALO:POOL_HALO + tm] = zc_ref[...]
    ext_ref[POOL_HALO + tm:] = jnp.where(pos0 + tm == seq, zero_halo, zn_ref[...])
    pos = pos0 + lax.broadcasted_iota(jnp.int32, (tm, 1), 0)
    lane_group = lax.broadcasted_iota(jnp.int32, (tm, LANES), 1) // (D_POOL // len(POOL_WINDOWS))

    def shifted(off, slab):
        return ext_ref[POOL_HALO + off:POOL_HALO + off + tm, slab * LANES:(slab + 1) * LANES]

    def window_mean(total, w):
        cnt = jnp.minimum(pos + w // 2, seq) - jnp.maximum(pos - w // 2, 0)
        return total / cnt.astype(F32)

    pooled = []
    for slab in range(2):
        w_small, w_big = POOL_WINDOWS[2 * slab], POOL_WINDOWS[2 * slab + 1]
        s_small = shifted(-(w_small // 2), slab)
        for off in range(-(w_small // 2) + 1, w_small // 2):
            s_small = s_small + shifted(off, slab)
        s_big = s_small
        for off in list(range(-(w_big // 2), -(w_small // 2))) + list(range(w_small // 2, w_big // 2)):
            s_big = s_big + shifted(off, slab)
        mean = jnp.where(lane_group == 0, window_mean(s_small, w_small), window_mean(s_big, w_big))
        pooled.append(mean - shifted(0, slab))
    pm = jnp.concatenate(pooled, axis=-1).astype(BF16)
    yb = jnp.dot(pm, wp_ref[...], preferred_element_type=F32) * ps_ref[...]

    acc = jnp.dot(ya.astype(BF16), wo_ref[0:D_ATTN, :], preferred_element_type=F32)
    acc = acc + jnp.dot(yb.astype(BF16), wo_ref[D_ATTN:D_ATTN + D_POOL, :], preferred_element_type=F32)
    acc = acc + jnp.dot(yc_ref[...], wo_ref[D_ATTN + D_POOL:, :], preferred_element_type=F32)
    out_ref[...] = x_ref[...] + acc


def _mix_out(x2d, outs, lses, zb, yc, expand, w_pool_bd, pool_scale, w_out, tm, seq):
    n = x2d.shape[0]
    const = lambda i: (0, 0)
    row = lambda i: (i, 0)
    hb = tm // POOL_HALO
    last = n // POOL_HALO - 1
    return pl.pallas_call(
        functools.partial(_mix_out_kernel, tm=tm, seq=seq),
        grid=(n // tm,),
        in_specs=[
            pl.BlockSpec((tm, D_MODEL), row),
            pl.BlockSpec((tm, D_ATTN), row), pl.BlockSpec((tm, D_ATTN), row), pl.BlockSpec((tm, D_ATTN), row),
            pl.BlockSpec((tm, LANES), row), pl.BlockSpec((tm, LANES), row), pl.BlockSpec((tm, LANES), row),
            pl.BlockSpec((POOL_HALO, D_POOL), lambda i: (jnp.maximum(i * hb - 1, 0), 0)),
            pl.BlockSpec((tm, D_POOL), row),
            pl.BlockSpec((POOL_HALO, D_POOL), lambda i: (jnp.minimum((i + 1) * hb, last), 0)),
            pl.BlockSpec((tm, D_GATE), row),
            pl.BlockSpec((LANES, D_ATTN), const),
            pl.BlockSpec((D_POOL, D_POOL), const),
            pl.BlockSpec((1, D_POOL), const),
            pl.BlockSpec((D_MODEL, D_MODEL), const),
        ],
        out_specs=pl.BlockSpec((tm, D_MODEL), row),
        out_shape=jax.ShapeDtypeStruct((n, D_MODEL), F32),
        scratch_shapes=[pltpu.VMEM((tm + 2 * POOL_HALO, D_POOL), F32)],
        compiler_params=_cparams("parallel"),
        name="mix_out",
    )(x2d, outs[0], outs[1], outs[2], lses[0], lses[1], lses[2], zb, zb, zb, yc,
      expand, w_pool_bd, pool_scale, w_out)


def _mem_kv_kernel(m_ref, g_ref, w_ref, kv_ref):
    mn = _rms(m_ref[...], g_ref[...]).astype(BF16)
    kv_ref[...] = jnp.dot(mn, w_ref[...], preferred_element_type=F32).astype(BF16)


def _mem_kv(mem2d, norm_g, w_ckv):
    n = mem2d.shape[0]
    tm = N_MEM
    return pl.pallas_call(
        _mem_kv_kernel,
        grid=(n // tm,),
        in_specs=[
            pl.BlockSpec((tm, D_MODEL), lambda i: (i, 0)),
            pl.BlockSpec((1, D_MODEL), lambda i: (0, 0)),
            pl.BlockSpec((D_MODEL, 2 * D_MODEL), lambda i: (0, 0)),
        ],
        out_specs=pl.BlockSpec((tm, 2 * D_MODEL), lambda i: (i, 0)),
        out_shape=jax.ShapeDtypeStruct((n, 2 * D_MODEL), BF16),
        compiler_params=_cparams("parallel"),
        name="mem_kv",
    )(mem2d, norm_g, w_ckv)


def _route(hf, wr_hi_ref, wr_lo_ref, br_ref):
    tm = hf.shape[0]
    hi = hf.astype(BF16)
    lo = (hf - hi.astype(F32)).astype(BF16)
    lg = (jnp.dot(hi, wr_hi_ref[...], preferred_element_type=F32)
          + jnp.dot(lo, wr_hi_ref[...], preferred_element_type=F32)
          + jnp.dot(hi, wr_lo_ref[...], preferred_element_type=F32)) + br_ref[...]
    lane_f = lax.broadcasted_iota(jnp.int32, (tm, LANES), 1).astype(F32)
    big = jnp.float32(LANES)
    is_group = lane_f < N_GROUPS
    g_logit = jnp.where(is_group, lg, NEG_INF)
    g_max = jnp.max(g_logit, axis=-1, keepdims=True)
    g_idx = jnp.min(jnp.where(is_group & (g_logit == g_max), lane_f, big), axis=-1, keepdims=True)
    g_w = 1.0 / jnp.sum(jnp.where(is_group, jnp.exp(g_logit - g_max), 0.0), axis=-1, keepdims=True)
    first = N_GROUPS + EXPERTS_PER_GROUP * g_idx
    sel = (lane_f >= first) & (lane_f < first + EXPERTS_PER_GROUP)
    v1 = jnp.max(jnp.where(sel, lg, NEG_INF), axis=-1, keepdims=True)
    i1 = jnp.min(jnp.where(sel & (lg == v1), lane_f, big), axis=-1, keepdims=True)
    sel2 = sel & (lane_f != i1)
    v2 = jnp.max(jnp.where(sel2, lg, NEG_INF), axis=-1, keepdims=True)
    i2 = jnp.min(jnp.where(sel2 & (lg == v2), lane_f, big), axis=-1, keepdims=True)
    e2 = jnp.exp(v2 - v1)
    w1 = g_w / (1.0 + e2)
    w2 = g_w * e2 / (1.0 + e2)
    a = jnp.minimum(i1, i2) - first
    b = jnp.maximum(i1, i2) - first
    pair = a * (7.0 - a) * 0.5 + (b - a - 1.0)
    bucket = g_idx * PAIRS_PER_GROUP + pair
    first_is_low = i1 < i2
    return bucket, jnp.where(first_is_low, w1, w2), jnp.where(first_is_low, w2, w1)


def _cross_kernel(x_ref, g_ref, wq_ref, kv_ref, wo_ref, gf_ref, wrh_ref, wrl_ref, br_ref, tri_ref,
                  xaug_ref, cnt_ref, carry_ref, *, tm):
    x = x_ref[...]
    hc = _rms(x, g_ref[...]).astype(BF16)
    q = jnp.dot(hc, wq_ref[...], preferred_element_type=F32).astype(BF16)
    heads = []
    for h in range(N_CROSS_HEADS):
        cols = slice(h * CROSS_HEAD_DIM, (h + 1) * CROSS_HEAD_DIM)
        k = kv_ref[0, :, cols]
        v = kv_ref[0, :, D_MODEL + h * CROSS_HEAD_DIM:D_MODEL + (h + 1) * CROSS_HEAD_DIM]
        s = lax.dot_general(q[:, cols], k, (((1,), (1,)), ((), ())), preferred_element_type=F32)
        s = s * (CROSS_HEAD_DIM ** -0.5)
        m = jnp.max(s, axis=-1, keepdims=True)
        pe = jnp.exp(s - m)
        den = jnp.sum(pe, axis=-1, keepdims=True)
        heads.append((jnp.dot(pe.astype(BF16), v, preferred_element_type=F32) / den).astype(BF16))
    o = jnp.concatenate(heads, axis=-1)
    x2 = x + jnp.dot(o, wo_ref[...], preferred_element_type=F32)
    xaug_ref[:, :D_MODEL] = x2

    @pl.when(pl.program_id(0) == 0)
    def _():
        carry_ref[...] = jnp.zeros_like(carry_ref)

    bucket, w_lo, w_hi = _route(_rms(x2, gf_ref[...]), wrh_ref, wrl_ref, br_ref)
    lane_f = lax.broadcasted_iota(jnp.int32, (tm, LANES), 1).astype(F32)
    onehot = lane_f == bucket
    incl = jnp.dot(tri_ref[...], onehot.astype(BF16), preferred_element_type=F32)
    rank = jnp.sum(jnp.where(onehot, incl - 1.0 + carry_ref[...], 0.0), axis=-1, keepdims=True)
    carry_ref[...] += jnp.sum(onehot.astype(F32), axis=0, keepdims=True)
    cnt_ref[...] = carry_ref[...]
    slab = jnp.where(lane_f == ROUTE_BUCKET, bucket, 0.0)
    slab = jnp.where(lane_f == ROUTE_RANK, rank, slab)
    slab = jnp.where(lane_f == ROUTE_W_LO, w_lo, slab)
    slab = jnp.where(lane_f == ROUTE_W_HI, w_hi, slab)
    xaug_ref[:, D_MODEL:] = slab


def _cross(x2d, norm_g, w_cq, kv, w_co, norm_ffn, wr_hi, wr_lo, b_r, tri, tm, seq):
    n = x2d.shape[0]
    const = lambda i: (0, 0)
    per_seq = seq // tm
    return pl.pallas_call(
        functools.partial(_cross_kernel, tm=tm),
        grid=(n // tm,),
        in_specs=[
            pl.BlockSpec((tm, D_MODEL), lambda i: (i, 0)),
            pl.BlockSpec((1, D_MODEL), const),
            pl.BlockSpec((D_MODEL, D_MODEL), const),
            pl.BlockSpec((1, N_MEM, 2 * D_MODEL), lambda i: (i // per_seq, 0, 0)),
            pl.BlockSpec((D_MODEL, D_MODEL), const),
            pl.BlockSpec((1, D_MODEL), const),
            pl.BlockSpec((D_MODEL, LANES), const),
            pl.BlockSpec((D_MODEL, LANES), const),
            pl.BlockSpec((1, LANES), const),
            pl.BlockSpec((tm, tm), const),
        ],
        out_specs=[
            pl.BlockSpec((tm, D_AUG), lambda i: (i, 0)),
            pl.BlockSpec((1, LANES), const),
        ],
        out_shape=[
            jax.ShapeDtypeStruct((n, D_AUG), F32),
            jax.ShapeDtypeStruct((1, LANES), F32),
        ],
        scratch_shapes=[pltpu.VMEM((1, LANES), F32)],
        compiler_params=_cparams("arbitrary"),
        name="cross",
    )(x2d, norm_g, w_cq, kv, w_co, norm_ffn, wr_hi, wr_lo, b_r, tri)


def _moe_plan(xaug, counts, n_tiles):
    n = xaug.shape[0]
    bucket = xaug[:, D_MODEL + ROUTE_BUCKET].astype(jnp.int32)
    rank = xaug[:, D_MODEL + ROUTE_RANK].astype(jnp.int32)
    cnt = counts[0, :N_BUCKETS].astype(jnp.int32)
    tiles = (cnt + MOE_TILE - 1) // MOE_TILE
    tile_end = jnp.cumsum(tiles)
    tile_start = tile_end - tiles
    dest = tile_start[bucket] * MOE_TILE + rank
    src = jnp.zeros((n_tiles * MOE_TILE,), jnp.int32).at[dest].set(jnp.arange(n, dtype=jnp.int32))
    t = jnp.arange(n_tiles, dtype=jnp.int32)
    used = t < tile_end[-1]
    tb = jnp.minimum(jnp.searchsorted(tile_end, t, side="right").astype(jnp.int32), N_BUCKETS - 1)
    tb = jnp.where(used, tb, tb[jnp.maximum(tile_end[-1] - 1, 0)])
    n_valid = jnp.where(used, jnp.clip(cnt[tb] - (t - tile_start[tb]) * MOE_TILE, 0, MOE_TILE), 0)
    n_valid = jnp.concatenate([n_valid, jnp.zeros((1,), jnp.int32)])
    group, pair = tb // PAIRS_PER_GROUP, tb % PAIRS_PER_GROUP
    pair_lo = jnp.array([0, 0, 0, 1, 1, 2], jnp.int32)
    pair_hi = jnp.array([1, 2, 3, 2, 3, 3], jnp.int32)
    e_lo = group * EXPERTS_PER_GROUP + pair_lo[pair]
    e_hi = group * EXPERTS_PER_GROUP + pair_hi[pair]
    return e_lo, e_hi, n_valid, src.reshape(n_tiles, 1, MOE_TILE)


def _moe_kernel(elo_ref, ehi_ref, nv_ref, src_ref, nsrc_ref, xaug_hbm, g_ref, wu0_ref, wu1_ref, wd0_ref, wd1_ref,
                gf_ref, out_hbm, xin, yout, sem_in, sem_out, *, final):
    i = pl.program_id(0)
    nv = nv_ref[i]
    slot = i % 2

    def gather(rows_ref, s):
        def body(g, c):
            for k in range(DMA_UNROLL):
                r = g * DMA_UNROLL + k
                pltpu.make_async_copy(xaug_hbm.at[pl.ds(rows_ref[0, 0, r], 1)], xin.at[s, pl.ds(r, 1)],
                                      sem_in.at[s]).start(priority=k % 2)
            return c
        lax.fori_loop(0, MOE_TILE // DMA_UNROLL, body, 0)

    def wait_scatter(s, count):
        @pl.when(count == MOE_TILE)
        def _():
            pltpu.make_async_copy(yout.at[s], out_hbm.at[pl.ds(0, MOE_TILE)], sem_out.at[s]).wait()

        @pl.when(count != MOE_TILE)
        def _():
            def body(r, c):
                pltpu.make_async_copy(yout.at[s, pl.ds(0, 1)], out_hbm.at[pl.ds(0, 1)], sem_out.at[s]).wait()
                return c
            lax.fori_loop(0, count, body, 0)

    @pl.when(nv > 0)
    def _():
        @pl.when(i == 0)
        def _():
            gather(src_ref, 0)

        @pl.when(nv_ref[i + 1] > 0)
        def _():
            gather(nsrc_ref, 1 - slot)

        pltpu.make_async_copy(xaug_hbm.at[pl.ds(0, MOE_TILE)], xin.at[slot], sem_in.at[slot]).wait()

        @pl.when(i >= 2)
        def _():
            wait_scatter(slot, nv_ref[i - 2])

        xa = xin[slot]
        xr = xa[:, :D_MODEL]
        slab = xa[:, D_MODEL:]
        lane = lax.broadcasted_iota(jnp.int32, (MOE_TILE, LANES), 1)
        w_lo = jnp.sum(jnp.where(lane == ROUTE_W_LO, slab, 0.0), axis=-1, keepdims=True)
        w_hi = jnp.sum(jnp.where(lane == ROUTE_W_HI, slab, 0.0), axis=-1, keepdims=True)
        hf = _rms(xr, g_ref[...]).astype(BF16)
        y = jnp.zeros((MOE_TILE, D_MODEL), F32)
        for wu_ref, wd_ref, w in ((wu0_ref, wd0_ref, w_lo), (wu1_ref, wd1_ref, w_hi)):
            gu = jnp.dot(hf, wu_ref[0], preferred_element_type=F32)
            gate_half, up_half = gu[:, :D_EXPERT], gu[:, D_EXPERT:]
            a = (gate_half * jax.nn.sigmoid(gate_half)) * up_half
            y = y + w * jnp.dot(a.astype(BF16), wd_ref[0], preferred_element_type=F32)
        res = xr + y
        if final:
            res = _rms(res, gf_ref[...])
        yout[slot] = res

        def scatter_row(r, c):
            pltpu.make_async_copy(yout.at[slot, pl.ds(r, 1)], out_hbm.at[pl.ds(src_ref[0, 0, r], 1)],
                                  sem_out.at[slot]).start()
            return c

        @pl.when(nv == MOE_TILE)
        def _():
            def scatter_rows(g, c):
                for k in range(DMA_UNROLL):
                    r = g * DMA_UNROLL + k
                    pltpu.make_async_copy(yout.at[slot, pl.ds(r, 1)], out_hbm.at[pl.ds(src_ref[0, 0, r], 1)],
                                          sem_out.at[slot]).start(priority=k % 2)
                return c
            lax.fori_loop(0, MOE_TILE // DMA_UNROLL, scatter_rows, 0)

        @pl.when(nv != MOE_TILE)
        def _():
            lax.fori_loop(0, nv, scatter_row, 0)

        @pl.when(nv_ref[i + 1] == 0)
        def _():
            @pl.when(i >= 1)
            def _():
                wait_scatter(1 - slot, nv_ref[i - 1])
            wait_scatter(slot, nv)


def _moe(xaug, counts, norm_g, w_up, w_down, norm_final, final):
    n = xaug.shape[0]
    n_tiles = n // MOE_TILE + N_BUCKETS
    e_lo, e_hi, n_valid, src = _moe_plan(xaug, counts, n_tiles)
    const = lambda i, lo, hi, nv: (0, 0)
    grid_spec = pltpu.PrefetchScalarGridSpec(
        num_scalar_prefetch=3,
        grid=(n_tiles,),
        in_specs=[
            pl.BlockSpec((1, 1, MOE_TILE), lambda i, lo, hi, nv: (i, 0, 0), memory_space=pltpu.SMEM),
            pl.BlockSpec((1, 1, MOE_TILE), lambda i, lo, hi, nv: (jnp.minimum(i + 1, n_tiles - 1), 0, 0),
                         memory_space=pltpu.SMEM),
            pl.BlockSpec(memory_space=pl.ANY),
            pl.BlockSpec((1, D_MODEL), const),
            pl.BlockSpec((1, D_MODEL, 2 * D_EXPERT), lambda i, lo, hi, nv: (lo[i], 0, 0)),
            pl.BlockSpec((1, D_MODEL, 2 * D_EXPERT), lambda i, lo, hi, nv: (hi[i], 0, 0)),
            pl.BlockSpec((1, D_EXPERT, D_MODEL), lambda i, lo, hi, nv: (lo[i], 0, 0)),
            pl.BlockSpec((1, D_EXPERT, D_MODEL), lambda i, lo, hi, nv: (hi[i], 0, 0)),
            pl.BlockSpec((1, D_MODEL), const),
        ],
        out_specs=pl.BlockSpec(memory_space=pl.ANY),
        scratch_shapes=[
            pltpu.VMEM((2, MOE_TILE, D_AUG), F32),
            pltpu.VMEM((2, MOE_TILE, D_MODEL), F32),
            pltpu.SemaphoreType.DMA((2,)),
            pltpu.SemaphoreType.DMA((2,)),
        ],
    )
    return pl.pallas_call(
        functools.partial(_moe_kernel, final=final),
        grid_spec=grid_spec,
        out_shape=jax.ShapeDtypeStruct((n, D_MODEL), F32),
        compiler_params=_cparams("arbitrary"),
        name="moe",
    )(e_lo, e_hi, n_valid, src, src, xaug, norm_g, w_up, w_up, w_down, w_down, norm_final)


def _prep_layer(p, l):
    b_full = jnp.repeat(jnp.transpose(p["b_spatial"][l]), D_GATE // N_GATE_GROUPS, axis=1)
    wp = p["w_pool"][l]
    gsz = D_POOL // len(POOL_WINDOWS)
    w_pool_bd = jnp.zeros((D_POOL, D_POOL), F32)
    for g in range(len(POOL_WINDOWS)):
        w_pool_bd = w_pool_bd.at[g * gsz:(g + 1) * gsz, g * gsz:(g + 1) * gsz].set(wp[g])
    w_r = jnp.zeros((D_MODEL, LANES), F32)
    w_r = w_r.at[:, :N_GROUPS].set(p["w_router_group"][l])
    w_r = w_r.at[:, N_GROUPS:N_GROUPS + N_EXPERTS].set(p["w_router_expert"][l])
    wr_hi = w_r.astype(BF16)
    wr_lo = (w_r - wr_hi.astype(F32)).astype(BF16)
    b_r = jnp.zeros((1, LANES), F32)
    b_r = b_r.at[0, :N_GROUPS].set(p["b_router_group"][l])
    b_r = b_r.at[0, N_GROUPS:N_GROUPS + N_EXPERTS].set(p["b_router_expert"][l])
    return dict(
        norm_mix=p["norm_mix"][l][None], w_in=p["w_in"][l].astype(BF16),
        gate_norm=p["gate_norm"][l][None], w_sp=p["w_spatial"][l].astype(BF16), b_full=b_full,
        w_pool_bd=w_pool_bd.astype(BF16), pool_scale=p["pool_scale"][l][None], w_out=p["w_out"][l].astype(BF16),
        norm_cross=p["norm_cross"][l][None], norm_mem=p["norm_mem"][l][None],
        w_cq=p["w_cq"][l].astype(BF16), w_ckv=p["w_ckv"][l].astype(BF16), w_co=p["w_co"][l].astype(BF16),
        norm_ffn=p["norm_ffn"][l][None], wr_hi=wr_hi, wr_lo=wr_lo, b_r=b_r,
        w_up=p["w_up"][l].astype(BF16), w_down=p["w_down"][l].astype(BF16),
    )


def _prepare(p):
    depth = p["norm_mix"].shape[0]
    head_of_lane = jnp.arange(D_ATTN) // HEAD_DIM
    return dict(
        layers=[_prep_layer(p, l) for l in range(depth)],
        biases=[_attn_bias(d) for d in DILATIONS],
        expand=(jnp.arange(LANES)[:, None] == head_of_lane[None, :]).astype(BF16),
        norm_final=p["norm_final"][None],
        tri=jnp.tril(jnp.ones((TOKEN_TILE, TOKEN_TILE), F32)).astype(BF16),
    )


def _trunk(x, mem, prep):
    batch, seq, _ = x.shape
    n = batch * seq
    x2d = x.reshape(n, D_MODEL)
    mem2d = mem.reshape(batch * N_MEM, D_MODEL)
    layers = prep["layers"]
    tm = TOKEN_TILE
    for li, lp in enumerate(layers):
        qkv, zb, yc = _mix_in(x2d, lp["norm_mix"], lp["w_in"], lp["gate_norm"], lp["w_sp"], lp["b_full"], tm)
        outs, lses = [], []
        for dil, bias in zip(DILATIONS, prep["biases"]):
            o, lse = _attn_pattern(qkv, bias, batch, seq, dil)
            outs.append(o)
            lses.append(lse)
        x2d = _mix_out(x2d, outs, lses, zb, yc, prep["expand"], lp["w_pool_bd"], lp["pool_scale"], lp["w_out"],
                       tm, seq)
        kv = _mem_kv(mem2d, lp["norm_mem"], lp["w_ckv"]).reshape(batch, N_MEM, 2 * D_MODEL)
        xaug, counts = _cross(x2d, lp["norm_cross"], lp["w_cq"], kv, lp["w_co"], lp["norm_ffn"], lp["wr_hi"],
                              lp["wr_lo"], lp["b_r"], prep["tri"], tm, seq)
        x2d = _moe(xaug, counts, lp["norm_ffn"], lp["w_up"], lp["w_down"], prep["norm_final"],
                   li == len(layers) - 1)
    return x2d.reshape(batch, seq, D_MODEL)


def kernel(x_prompt, x_sample, mem_prompt, mem_sample, norm_mix, w_in, w_pool, pool_scale, gate_norm, w_spatial, b_spatial, w_out, norm_cross, norm_mem, w_cq, w_ckv, w_co, norm_ffn, w_router_group, b_router_group, w_router_expert, b_router_expert, w_up, w_down, norm_final):
    prep = _prepare(dict(
        norm_mix=norm_mix, w_in=w_in, w_pool=w_pool, pool_scale=pool_scale, gate_norm=gate_norm,
        w_spatial=w_spatial, b_spatial=b_spatial, w_out=w_out, norm_cross=norm_cross, norm_mem=norm_mem,
        w_cq=w_cq, w_ckv=w_ckv, w_co=w_co, norm_ffn=norm_ffn, w_router_group=w_router_group,
        b_router_group=b_router_group, w_router_expert=w_router_expert, b_router_expert=b_router_expert,
        w_up=w_up, w_down=w_down, norm_final=norm_final))
    return (_trunk(x_prompt, mem_prompt, prep), _trunk(x_sample, mem_sample, prep))
```
